```python
import math
import jax
import jax.numpy as jnp
from jax import lax
import numpy as np

D_MODEL = 4096
BATCH = 1
SEQ = 16384
DEPTH = 4

CHUNK = 64
WINDOW = 128
ATTN_BLOCK = 128
HEAD_DIM = 128
N_Q_HEADS = D_MODEL // 256
N_KV_HEADS = max(1, N_Q_HEADS // 8)
GQA_GROUP = N_Q_HEADS // N_KV_HEADS
ATTN_WIDTH = N_Q_HEADS * HEAD_DIM
KV_WIDTH = N_KV_HEADS * HEAD_DIM
ROT_DIM = HEAD_DIM // 4
ROPE_THETA = 500000.0
SSM_WIDTH = D_MODEL // 2
SSM_GROUP_CH = 16
SSM_GROUPS = SSM_WIDTH // SSM_GROUP_CH
SSM_STATE = 64
SSM_DT_MIN = 1e-3
SSM_DT_MAX = 1e-1
D_FF = D_MODEL
N_MEM = 256
XA_HEADS = 4
XA_HEAD_DIM = 128
XA_WIDTH = XA_HEADS * XA_HEAD_DIM
IN_COLS = ATTN_WIDTH + 2 * KV_WIDTH + SSM_WIDTH + 2 * D_MODEL
N_GAINS = 8
RMS_EPS = 1e-6
MASK_VALUE = -1e30

kernel_name = 'hybrid_swa_s5_macaron_stream_encoder'


def rmsnorm(x, gain):
    xf = x.astype(jnp.float32)
    y = xf * lax.rsqrt(jnp.mean(xf * xf, axis=-1, keepdims=True) + RMS_EPS)
    return (y * gain.astype(jnp.float32)).astype(x.dtype)


def swiglu(x, w_gu, w_down):
    gate, up = jnp.split(x @ w_gu, 2, axis=-1)
    return (jax.nn.silu(gate) * up) @ w_down


def rope_tables(positions):
    inv_freq = ROPE_THETA ** (-jnp.arange(0, ROT_DIM, 2, dtype=jnp.float32) / ROT_DIM)
    ang = positions.astype(jnp.float32)[..., None] * inv_freq
    return jnp.cos(ang)[:, :, None, :], jnp.sin(ang)[:, :, None, :]


def apply_partial_rope(t, cos, sin):
    half = ROT_DIM // 2
    tr = t[..., :ROT_DIM].astype(jnp.float32)
    t1, t2 = tr[..., :half], tr[..., half:]
    rot = jnp.concatenate([t1 * cos - t2 * sin, t2 * cos + t1 * sin], axis=-1)
    return jnp.concatenate([rot.astype(t.dtype), t[..., ROT_DIM:]], axis=-1)


def band_blocks(t):
    b, l = t.shape[:2]
    blk = t.reshape(b, l // ATTN_BLOCK, ATTN_BLOCK, *t.shape[2:])
    prev = jnp.concatenate([jnp.zeros_like(blk[:, :1]), blk[:, :-1]], axis=1)
    return jnp.concatenate([prev, blk], axis=2)


def sliding_window_gqa_with_sinks(q, k, v, sinks):
    b, l = q.shape[:2]
    nb = l // ATTN_BLOCK
    cpb = ATTN_BLOCK // CHUNK
    win_chunks = WINDOW // CHUNK
    qb = q.reshape(b, nb, ATTN_BLOCK, N_KV_HEADS, GQA_GROUP, HEAD_DIM)
    kb, vb = band_blocks(k), band_blocks(v)
    s = jnp.einsum('bnqkgd,bnskd->bnkgqs', qb, kb).astype(jnp.float32) * (HEAD_DIM ** -0.5)
    q_chunk = jnp.arange(ATTN_BLOCK) // CHUNK
    k_chunk = jnp.arange(2 * ATTN_BLOCK) // CHUNK
    band = (k_chunk[None, :] >= q_chunk[:, None] + cpb - win_chunks) & (k_chunk[None, :] <= q_chunk[:, None] + cpb)
    k_pos = jnp.arange(nb)[:, None] * ATTN_BLOCK - ATTN_BLOCK + jnp.arange(2 * ATTN_BLOCK)[None, :]
    mask = band[None] & (k_pos >= 0)[:, None, :]
    s = jnp.where(mask[None, :, None, None], s, MASK_VALUE)
    sink = sinks.astype(jnp.float32).reshape(N_KV_HEADS, GQA_GROUP)[None, None, :, :, None]
    m = jnp.maximum(jnp.max(s, axis=-1), sink)
    p = jnp.exp(s - m[..., None])
    probs = p / (jnp.sum(p, axis=-1) + jnp.exp(sink - m))[..., None]
    o = jnp.einsum('bnkgqs,bnskd->bnqkgd', probs.astype(v.dtype), vb)
    return o.reshape(b, l, ATTN_WIDTH)


def s5_glu_branch(u, lam_re, lam_im, log_dt, b_re, b_im, c_re, c_im, d_skip, w_glu):
    b, l = u.shape[:2]
    f32 = jnp.float32
    ug = u.astype(f32).reshape(b, l, SSM_GROUPS, SSM_GROUP_CH)
    lam = lax.complex(jnp.minimum(lam_re.astype(f32), -1e-4), lam_im.astype(f32))
    dt = jnp.exp(log_dt.astype(f32))[:, None]
    log_lam_bar = lam * dt
    lam_bar = jnp.exp(log_lam_bar)
    b_bar = ((lam_bar - 1.0) / lam)[:, :, None] * lax.complex(b_re.astype(f32), b_im.astype(f32))
    bu = jnp.einsum('gph,blgh->blgp', b_bar, ug.astype(jnp.complex64))
    steps = jnp.ones((1, l, 1, 1), f32)

    def combine(earlier, later):
        n_e, x_e = earlier
        n_l, x_l = later
        return n_e + n_l, jnp.exp(n_l * log_lam_bar) * x_e + x_l

    _, states = lax.associative_scan(combine, (steps, bu), axis=1)
    c = lax.complex(c_re.astype(f32), c_im.astype(f32))
    y = jnp.einsum('ghp,blgp->blgh', c, states).real + d_skip.astype(f32) * ug
    y = jax.nn.gelu(y.reshape(b, l, SSM_WIDTH)).astype(u.dtype)
    return y * jax.nn.sigmoid(y @ w_glu)


def memory_cross_attention(xn, memn, wq, wkv, wo):
    b, l = xn.shape[:2]
    q = (xn @ wq).reshape(b, l, XA_HEADS, XA_HEAD_DIM)
    k, v = jnp.split(memn @ wkv, 2, axis=-1)
    k = k.reshape(b, -1, XA_HEADS, XA_HEAD_DIM)
    v = v.reshape(b, -1, XA_HEADS, XA_HEAD_DIM)
    s = jnp.einsum('blhd,bmhd->bhlm', q, k).astype(jnp.float32) * (XA_HEAD_DIM ** -0.5)
    p = jax.nn.softmax(s, axis=-1).astype(v.dtype)
    o = jnp.einsum('bhlm,bmhd->blhd', p, v).reshape(b, l, XA_WIDTH)
    return o @ wo


def setup_inputs(seed: int = 0) -> dict:
    key = jax.random.key(seed)
    ks = jax.random.split(key, 32)
    f32 = jnp.float32

    def nrm(k, shape, fan_in):
        return jax.random.normal(k, shape, f32) * (fan_in ** -0.5)

    x = jax.random.normal(ks[0], (BATCH, SEQ, D_MODEL), f32)
    mem = jax.random.normal(ks[1], (BATCH, N_MEM, D_MODEL), f32)
    offset = jax.random.randint(ks[2], (BATCH, 1), 0, 64, dtype=jnp.int32) * CHUNK
    positions = (offset + jnp.arange(SEQ, dtype=jnp.int32)[None, :]).astype(jnp.int32)
    norm_gains = 1.0 + 0.05 * jax.random.normal(ks[3], (DEPTH, N_GAINS, D_MODEL), f32)
    mem_norm_gain = 1.0 + 0.05 * jax.random.normal(ks[4], (DEPTH, D_MODEL), f32)
    ffn1_w_gu = nrm(ks[5], (DEPTH, D_MODEL, 2 * D_FF), D_MODEL)
    ffn1_w_down = nrm(ks[6], (DEPTH, D_FF, D_MODEL), D_FF)
    w_in = nrm(ks[7], (DEPTH, D_MODEL, IN_COLS), D_MODEL)
    attn_sinks = jax.random.normal(ks[8], (DEPTH, N_Q_HEADS), f32)
    w_attn_out = nrm(ks[9], (DEPTH, ATTN_WIDTH, D_MODEL), ATTN_WIDTH)
    ssm_lambda_re = -0.5 + 0.01 * jax.random.normal(ks[10], (DEPTH, SSM_GROUPS, SSM_STATE), f32)
    ssm_lambda_im = math.pi * jnp.arange(SSM_STATE, dtype=f32)[None, None, :] + 0.01 * jax.random.normal(ks[11], (DEPTH, SSM_GROUPS, SSM_STATE), f32)
    ssm_log_dt = jax.random.uniform(ks[12], (DEPTH, SSM_GROUPS), f32, minval=math.log(SSM_DT_MIN), maxval=math.log(SSM_DT_MAX))
    ssm_b_re = nrm(ks[13], (DEPTH, SSM_GROUPS, SSM_STATE, SSM_GROUP_CH), 2 * SSM_GROUP_CH)
    ssm_b_im = nrm(ks[14], (DEPTH, SSM_GROUPS, SSM_STATE, SSM_GROUP_CH), 2 * SSM_GROUP_CH)
    ssm_c_re = nrm(ks[15], (DEPTH, SSM_GROUPS, SSM_GROUP_CH, SSM_STATE), 2 * SSM_STATE)
    ssm_c_im = nrm(ks[16], (DEPTH, SSM_GROUPS, SSM_GROUP_CH, SSM_STATE), 2 * SSM_STATE)
    ssm_d = jax.random.normal(ks[17], (DEPTH, SSM_GROUPS, SSM_GROUP_CH), f32)
    w_glu = nrm(ks[18], (DEPTH, SSM_WIDTH, SSM_WIDTH), SSM_WIDTH)
    w_ssm_out = nrm(ks[19], (DEPTH, SSM_WIDTH, D_MODEL), SSM_WIDTH)
    w_o = nrm(ks[20], (DEPTH, D_MODEL, D_MODEL), D_MODEL)
    xa_wq = nrm(ks[21], (DEPTH, D_MODEL, XA_WIDTH), D_MODEL)
    xa_wkv = nrm(ks[22], (DEPTH, D_MODEL, 2 * XA_WIDTH), D_MODEL)
    xa_wo = nrm(ks[23], (DEPTH, XA_WIDTH, D_MODEL), XA_WIDTH)
    ffn2_w_gu = nrm(ks[24], (DEPTH, D_MODEL, 2 * D_FF), D_MODEL)
    ffn2_w_down = nrm(ks[25], (DEPTH, D_FF, D_MODEL), D_FF)
    return {'x': x, 'mem': mem, 'positions': positions, 'norm_gains': norm_gains,
            'mem_norm_gain': mem_norm_gain, 'ffn1_w_gu': ffn1_w_gu, 'ffn1_w_down': ffn1_w_down,
            'w_in': w_in, 'attn_sinks': attn_sinks, 'w_attn_out': w_attn_out,
            'ssm_lambda_re': ssm_lambda_re, 'ssm_lambda_im': ssm_lambda_im, 'ssm_log_dt': ssm_log_dt,
            'ssm_b_re': ssm_b_re, 'ssm_b_im': ssm_b_im, 'ssm_c_re': ssm_c_re, 'ssm_c_im': ssm_c_im,
            'ssm_d': ssm_d, 'w_glu': w_glu, 'w_ssm_out': w_ssm_out, 'w_o': w_o,
            'xa_wq': xa_wq, 'xa_wkv': xa_wkv, 'xa_wo': xa_wo,
            'ffn2_w_gu': ffn2_w_gu, 'ffn2_w_down': ffn2_w_down}


def reference(x, mem, positions, norm_gains, mem_norm_gain, ffn1_w_gu, ffn1_w_down, w_in, attn_sinks,
              w_attn_out, ssm_lambda_re, ssm_lambda_im, ssm_log_dt, ssm_b_re, ssm_b_im, ssm_c_re, ssm_c_im,
              ssm_d, w_glu, w_ssm_out, w_o, xa_wq, xa_wkv, xa_wo, ffn2_w_gu, ffn2_w_down):
    b, l = x.shape[:2]
    cos, sin = rope_tables(positions)
    c0 = ATTN_WIDTH
    c1 = c0 + KV_WIDTH
    c2 = c1 + KV_WIDTH
    c3 = c2 + SSM_WIDTH
    c4 = c3 + D_MODEL
    h = x
    for i in range(DEPTH):
        g = norm_gains[i]
        h = h + 0.5 * rmsnorm(swiglu(rmsnorm(h, g[0]), ffn1_w_gu[i], ffn1_w_down[i]), g[1])
        u = rmsnorm(h, g[2])
        q, k, v, s_in, gate_a, gate_s = jnp.split(u @ w_in[i], [c0, c1, c2, c3, c4], axis=-1)
        q = apply_partial_rope(q.reshape(b, l, N_Q_HEADS, HEAD_DIM), cos, sin)
        k = apply_partial_rope(k.reshape(b, l, N_KV_HEADS, HEAD_DIM), cos, sin)
        v = v.reshape(b, l, N_KV_HEADS, HEAD_DIM)
        y_attn = sliding_window_gqa_with_sinks(q, k, v, attn_sinks[i]) @ w_attn_out[i]
        y_ssm = s5_glu_branch(s_in, ssm_lambda_re[i], ssm_lambda_im[i], ssm_log_dt[i], ssm_b_re[i], ssm_b_im[i],
                              ssm_c_re[i], ssm_c_im[i], ssm_d[i], w_glu[i]) @ w_ssm_out[i]
        mixed = (jax.nn.sigmoid(gate_a) * y_attn + jax.nn.sigmoid(gate_s) * y_ssm) @ w_o[i]
        h = h + rmsnorm(mixed, g[3])
        memn = rmsnorm(mem, mem_norm_gain[i])
        h = h + rmsnorm(memory_cross_attention(rmsnorm(h, g[4]), memn, xa_wq[i], xa_wkv[i], xa_wo[i]), g[5])
        h = h + 0.5 * rmsnorm(swiglu(rmsnorm(h, g[6]), ffn2_w_gu[i], ffn2_w_down[i]), g[7])
    return h
```

```python
import functools
import math

import jax
import jax.numpy as jnp
from jax import lax
from jax.experimental import pallas as pl
from jax.experimental.pallas import tpu as pltpu

F32 = jnp.float32
BF16 = jnp.bfloat16

CHUNK = 64
WINDOW = 128
ATTN_BLOCK = 128
HEAD_DIM = 128
ROT_DIM = HEAD_DIM // 4
ROPE_THETA = 500000.0
SSM_GROUP_CH = 16
SSM_STATE = 64
XA_HEADS = 4
XA_HEAD_DIM = 128
RMS_EPS = 1e-6
MASK_VALUE = -1e30

V7X_LANES = 128
V7X_SUBLANES = 8
V7X_VMEM_LIMIT_BYTES = 52 * 1024 * 1024

GROUPS_PER_LANE_BLOCK = V7X_LANES // SSM_GROUP_CH
STATE_COLS = GROUPS_PER_LANE_BLOCK * SSM_STATE


def _pow2_tile(limit, *sizes):
    t = 1
    while t * 2 <= limit and all(s % (t * 2) == 0 for s in sizes):
        t *= 2
    return t


def _params(*sem):
    return pltpu.CompilerParams(dimension_semantics=sem, vmem_limit_bytes=V7X_VMEM_LIMIT_BYTES)


def _rms(x, g):
    ms = jnp.mean(x * x, axis=-1, keepdims=True)
    return x * lax.rsqrt(ms + RMS_EPS) * g


def _dot(a, b):
    return jnp.dot(a, b, preferred_element_type=F32)


def _dot_nt(a, b):
    return lax.dot_general(a, b, (((1,), (1,)), ((), ())), preferred_element_type=F32)


def _norm_kernel(x_ref, g_ref, o_ref):
    o_ref[...] = _rms(x_ref[...], g_ref[...]).astype(o_ref.dtype)


def rmsnorm_bf16(x, g):
    m, d = x.shape
    bm = _pow2_tile(256, m)
    return pl.pallas_call(
        _norm_kernel,
        grid=(m // bm,),
        in_specs=[pl.BlockSpec((bm, d), lambda i: (i, 0)), pl.BlockSpec((1, d), lambda i: (0, 0))],
        out_specs=pl.BlockSpec((bm, d), lambda i: (i, 0)),
        out_shape=jax.ShapeDtypeStruct((m, d), BF16),
        compiler_params=_params("parallel"),
        name="rmsnorm",
    )(x, g.reshape(1, d))


def _resnorm_kernel(y_ref, h_ref, gpost_ref, gpre_ref, h_out, a_out, *, scale):
    hn = h_ref[...] + scale * _rms(y_ref[...], gpost_ref[...])
    h_out[...] = hn
    a_out[...] = _rms(hn, gpre_ref[...]).astype(a_out.dtype)


def _res_kernel(y_ref, h_ref, gpost_ref, h_out, *, scale):
    h_out[...] = h_ref[...] + scale * _rms(y_ref[...], gpost_ref[...])


def residual_norm(y, h, g_post, g_pre, scale):
    m, d = h.shape
    bm = _pow2_tile(256, m)
    row = pl.BlockSpec((bm, d), lambda i: (i, 0))
    gain = pl.BlockSpec((1, d), lambda i: (0, 0))
    if g_pre is None:
        return pl.pallas_call(
            functools.partial(_res_kernel, scale=scale),
            grid=(m // bm,),
            in_specs=[row, row, gain],
            out_specs=row,
            out_shape=jax.ShapeDtypeStruct((m, d), F32),
            compiler_params=_params("parallel"),
            name="residual",
        )(y, h, g_post.reshape(1, d)), None
    return pl.pallas_call(
        functools.partial(_resnorm_kernel, scale=scale),
        grid=(m // bm,),
        in_specs=[row, row, gain, gain],
        out_specs=[row, row],
        out_shape=[jax.ShapeDtypeStruct((m, d), F32), jax.ShapeDtypeStruct((m, d), BF16)],
        compiler_params=_params("parallel"),
        name="residual_norm",
    )(y, h, g_post.reshape(1, d), g_pre.reshape(1, d))


def _mm_kernel(a_ref, w_ref, o_ref):
    o_ref[...] = _dot(a_ref[...], w_ref[...]).astype(o_ref.dtype)


def matmul(a, w, col0, ncols, out_dtype, name):
    m, k = a.shape
    bm = _pow2_tile(1024, m)
    bn = _pow2_tile(512, col0, ncols) if col0 else _pow2_tile(512, ncols)
    off = col0 // bn
    return pl.pallas_call(
        _mm_kernel,
        grid=(m // bm, ncols // bn),
        in_specs=[pl.BlockSpec((bm, k), lambda i, j: (i, 0)),
                  pl.BlockSpec((k, bn), lambda i, j: (0, j + off))],
        out_specs=pl.BlockSpec((bm, bn), lambda i, j: (i, j)),
        out_shape=jax.ShapeDtypeStruct((m, ncols), out_dtype),
        compiler_params=_params("parallel", "arbitrary"),
        name=name,
    )(a, w)


def _swiglu_kernel(a_ref, wg_ref, wu_ref, o_ref):
    a = a_ref[...]
    gate = _dot(a, wg_ref[...])
    up = _dot(a, wu_ref[...])
    o_ref[...] = (gate * jax.nn.sigmoid(gate) * up).astype(o_ref.dtype)


def swiglu_up(a, w_gu):
    m, k = a.shape
    f = w_gu.shape[1] // 2
    bm = _pow2_tile(1024, m)
    bn = _pow2_tile(512, f)
    up_off = f // bn
    return pl.pallas_call(
        _swiglu_kernel,
        grid=(m // bm, f // bn),
        in_specs=[pl.BlockSpec((bm, k), lambda i, j: (i, 0)),
                  pl.BlockSpec((k, bn), lambda i, j: (0, j)),
                  pl.BlockSpec((k, bn), lambda i, j: (0, j + up_off))],
        out_specs=pl.BlockSpec((bm, bn), lambda i, j: (i, j)),
        out_shape=jax.ShapeDtypeStruct((m, f), BF16),
        compiler_params=_params("parallel", "arbitrary"),
        name="swiglu_up",
    )(a, w_gu, w_gu)


def _glu_kernel(y_ref, yj_ref, w_ref, o_ref):
    z = _dot(y_ref[...].astype(BF16), w_ref[...])
    o_ref[...] = (yj_ref[...] * jax.nn.sigmoid(z)).astype(o_ref.dtype)


def glu(y, w):
    m, k = y.shape
    n = w.shape[1]
    bm = _pow2_tile(1024, m)
    bn = _pow2_tile(512, n)
    return pl.pallas_call(
        _glu_kernel,
        grid=(m // bm, n // bn),
        in_specs=[pl.BlockSpec((bm, k), lambda i, j: (i, 0)),
                  pl.BlockSpec((bm, bn), lambda i, j: (i, j)),
                  pl.BlockSpec((k, bn), lambda i, j: (0, j))],
        out_specs=pl.BlockSpec((bm, bn), lambda i, j: (i, j)),
        out_shape=jax.ShapeDtypeStruct((m, n), BF16),
        compiler_params=_params("parallel", "arbitrary"),
        name="ssm_glu",
    )(y, y, w)


def _mix_kernel(ao_ref, gl_ref, wa_ref, ws_ref, ga_ref, gs_ref, o_ref):
    y_attn = _dot(ao_ref[...], wa_ref[...])
    y_ssm = _dot(gl_ref[...], ws_ref[...])
    o_ref[...] = (jax.nn.sigmoid(ga_ref[...]) * y_attn
                  + jax.nn.sigmoid(gs_ref[...]) * y_ssm).astype(o_ref.dtype)


def gated_mix(attn_o, glu_o, w_attn_out, w_ssm_out, gates):
    m, ka = attn_o.shape
    ks = glu_o.shape[1]
    d = w_attn_out.shape[1]
    bm = _pow2_tile(1024, m)
    bn = _pow2_tile(512, d)
    gs_off = d // bn
    return pl.pallas_call(
        _mix_kernel,
        grid=(m // bm, d // bn),
        in_specs=[pl.BlockSpec((bm, ka), lambda i, j: (i, 0)),
                  pl.BlockSpec((bm, ks), lambda i, j: (i, 0)),
                  pl.BlockSpec((ka, bn), lambda i, j: (0, j)),
                  pl.BlockSpec((ks, bn), lambda i, j: (0, j)),
                  pl.BlockSpec((bm, bn), lambda i, j: (i, j)),
                  pl.BlockSpec((bm, bn), lambda i, j: (i, j + gs_off))],
        out_specs=pl.BlockSpec((bm, bn), lambda i, j: (i, j)),
        out_shape=jax.ShapeDtypeStruct((m, d), BF16),
        compiler_params=_params("parallel", "arbitrary"),
        name="gated_mix",
    )(attn_o, glu_o, w_attn_out, w_ssm_out, gates, gates)


def _rope_table_kernel(pos_ref, invf_ref, a_ref, b_ref, c_ref):
    ang = pos_ref[...].astype(F32) * invf_ref[...]
    cos, sin = jnp.cos(ang), jnp.sin(ang)
    lane = lax.broadcasted_iota(jnp.int32, ang.shape, 1)
    half = ROT_DIM // 2
    a_ref[...] = jnp.where(lane < ROT_DIM, cos, 1.0)
    b_ref[...] = jnp.where(lane < half, -sin, 0.0)
    c_ref[...] = jnp.where((lane >= half) & (lane < ROT_DIM), sin, 0.0)


def rope_tables(positions):
    l = positions.shape[0]
    half = ROT_DIM // 2
    inv_freq = ROPE_THETA ** (-jnp.arange(0, ROT_DIM, 2, dtype=F32) / ROT_DIM)
    invf = jnp.concatenate([inv_freq, inv_freq, jnp.zeros((HEAD_DIM - ROT_DIM,), F32)]).reshape(1, HEAD_DIM)
    bm = _pow2_tile(1024, l)
    tab = pl.BlockSpec((bm, HEAD_DIM), lambda i: (i, 0))
    shape = jax.ShapeDtypeStruct((l, HEAD_DIM), F32)
    assert half * 2 == ROT_DIM
    return pl.pallas_call(
        _rope_table_kernel,
        grid=(l // bm,),
        in_specs=[pl.BlockSpec((bm, 1), lambda i: (i, 0)), pl.BlockSpec((1, HEAD_DIM), lambda i: (0, 0))],
        out_specs=[tab, tab, tab],
        out_shape=[shape, shape, shape],
        compiler_params=_params("parallel"),
        name="rope_tables",
    )(positions.reshape(l, 1), invf)


def _rope(x, a, b, c):
    half = ROT_DIM // 2
    return x * a + pltpu.roll(x, HEAD_DIM - half, 1) * b + pltpu.roll(x, half, 1) * c


def _swa_kernel(sink_ref, q_ref, kp_ref, kc_ref, vp_ref, vc_ref, ap_ref, bp_ref, cp_ref,
                ac_ref, bc_ref, cc_ref, o_ref, *, n_kv, group):
    i = pl.program_id(0)
    blk = ATTN_BLOCK
    rows = group * blk
    ac, bc, cc = ac_ref[...], bc_ref[...], cc_ref[...]
    ap, bp, cp = ap_ref[...], bp_ref[...], cp_ref[...]

    q_idx = lax.broadcasted_iota(jnp.int32, (rows, 2 * blk), 0) & (blk - 1)
    k_idx = lax.broadcasted_iota(jnp.int32, (rows, 2 * blk), 1)
    chunk_shift = CHUNK.bit_length() - 1
    q_chunk = q_idx >> chunk_shift
    k_chunk = k_idx >> chunk_shift
    cpb = blk // CHUNK
    win = WINDOW // CHUNK
    band = (k_chunk >= q_chunk + cpb - win) & (k_chunk <= q_chunk + cpb)
    first_valid_key = jnp.where(i > 0, 0, blk)
    mask = band & (k_idx >= first_valid_key)
    scale = HEAD_DIM ** -0.5

    for kh in range(n_kv):
        ks = slice(kh * HEAD_DIM, (kh + 1) * HEAD_DIM)
        k = jnp.concatenate([_rope(kp_ref[:, ks], ap, bp, cp), _rope(kc_ref[:, ks], ac, bc, cc)], axis=0)
        v = jnp.concatenate([vp_ref[:, ks], vc_ref[:, ks]], axis=0).astype(BF16)
        heads = [kh * group + g for g in range(group)]
        q = jnp.concatenate([_rope(q_ref[:, h * HEAD_DIM:(h + 1) * HEAD_DIM], ac, bc, cc) for h in heads], axis=0)
        sink = jnp.concatenate([jnp.full((blk, 1), sink_ref[0, h], F32) for h in heads], axis=0)
        s = _dot_nt(q.astype(BF16), k.astype(BF16)) * scale
        s = jnp.where(mask, s, MASK_VALUE)
        m = jnp.maximum(jnp.max(s, axis=-1, keepdims=True), sink)
        p = jnp.exp(s - m)
        den = jnp.sum(p, axis=-1, keepdims=True) + jnp.exp(sink - m)
        o = _dot((p / den).astype(BF16), v)
        for g, h in enumerate(heads):
            o_ref[:, h * HEAD_DIM:(h + 1) * HEAD_DIM] = o[g * blk:(g + 1) * blk].astype(o_ref.dtype)


def sliding_window_attention(qkv, sinks, tabs, n_q, n_kv):
    l = qkv.shape[0]
    aw, kw = n_q * HEAD_DIM, n_kv * HEAD_DIM
    blk = ATTN_BLOCK
    k_col, v_col = aw // kw, aw // kw + 1
    assert aw % kw == 0
    prev = lambda i: jnp.maximum(i - 1, 0)
    tab_c = pl.BlockSpec((blk, HEAD_DIM), lambda i: (i, 0))
    tab_p = pl.BlockSpec((blk, HEAD_DIM), lambda i: (prev(i), 0))
    ta, tb, tc = tabs
    return pl.pallas_call(
        functools.partial(_swa_kernel, n_kv=n_kv, group=n_q // n_kv),
        grid=(l // blk,),
        in_specs=[pl.BlockSpec(memory_space=pltpu.SMEM),
                  pl.BlockSpec((blk, aw), lambda i: (i, 0)),
                  pl.BlockSpec((blk, kw), lambda i: (prev(i), k_col)),
                  pl.BlockSpec((blk, kw), lambda i: (i, k_col)),
                  pl.BlockSpec((blk, kw), lambda i: (prev(i), v_col)),
                  pl.BlockSpec((blk, kw), lambda i: (i, v_col)),
                  tab_p, tab_p, tab_p, tab_c, tab_c, tab_c],
        out_specs=pl.BlockSpec((blk, aw), lambda i: (i, 0)),
        out_shape=jax.ShapeDtypeStruct((l, aw), BF16),
        compiler_params=_params("parallel"),
        name="swa_attention",
    )(sinks.reshape(1, n_q), qkv, qkv, qkv, qkv, qkv, ta, tb, tc, ta, tb, tc)


def _ssm_prep_kernel(lre_ref, lim_ref, ldt_ref, bre_ref, bim_ref,
                     lbr_ref, lbi_ref, lsr_ref, lsi_ref, bbr_ref, bbi_ref, *, sub_len):
    lr = jnp.minimum(lre_ref[...], -1e-4)
    li = lim_ref[...]
    dt = jnp.exp(ldt_ref[...])
    ar, ai = lr * dt, li * dt
    mag = jnp.exp(ar)
    lbr, lbi = mag * jnp.cos(ai), mag * jnp.sin(ai)
    lbr_ref[...] = lbr
    lbi_ref[...] = lbi
    mag_s = jnp.exp(sub_len * ar)
    lsr_ref[...] = mag_s * jnp.cos(sub_len * ai)
    lsi_ref[...] = mag_s * jnp.sin(sub_len * ai)
    nr, ni = lbr - 1.0, lbi
    den = lr * lr + li * li
    cr = (nr * lr + ni * li) / den
    ci = (ni * lr - nr * li) / den
    bre, bim = bre_ref[...], bim_ref[...]
    bbr_ref[...] = cr * bre - ci * bim
    bbi_ref[...] = cr * bim + ci * bre


def ssm_prepare(lam_re, lam_im, log_dt, b_re, b_im, sub_len):
    g, p = lam_re.shape
    h = b_re.shape[-1]
    gp = jax.ShapeDtypeStruct((g, 1, p), F32)
    ghp = jax.ShapeDtypeStruct((g, h, p), F32)
    return pl.pallas_call(
        functools.partial(_ssm_prep_kernel, sub_len=float(sub_len)),
        out_shape=[gp, gp, gp, gp, ghp, ghp],
        name="ssm_prepare",
    )(lam_re.reshape(g, 1, p), lam_im.reshape(g, 1, p), log_dt.reshape(g, 1, 1),
      b_re.transpose(0, 2, 1), b_im.transpose(0, 2, 1))


def _block_diag(w):
    g, a, b = w.shape
    n = GROUPS_PER_LANE_BLOCK
    w4 = w.reshape(g // n, n, a, b)
    eye = jnp.eye(n, dtype=w.dtype)
    return (w4[:, :, :, None, :] * eye[None, :, None, :, None]).reshape(g // n, n * a, n * b)


def _gelu_tanh(x):
    return 0.5 * x * (1.0 + jnp.tanh(math.sqrt(2.0 / math.pi) * (x + 0.044715 * (x * x * x))))


def _ssm_kernel(u_ref, bmat_ref, cmat_ref, lam_ref, lams_ref, d_ref, y_ref,
                perm_ref, x_ref, carry_ref, *, chunk):
    sub = V7X_SUBLANES
    sub_len = chunk // sub
    n = STATE_COLS

    @pl.when(pl.program_id(1) == 0)
    def _():
        carry_ref[...] = jnp.zeros_like(carry_ref)

    def permute_in(j, _):
        perm_ref[pl.ds(pl.multiple_of(j * sub, sub), sub), :] = u_ref[pl.ds(j, sub, stride=sub_len), :]
        return 0
    lax.fori_loop(0, sub_len, permute_in, 0)

    x_ref[...] = _dot(perm_ref[...].astype(BF16), bmat_ref[0])

    lam_r = jnp.broadcast_to(lam_ref[0, :, 0:n], (sub, n))
    lam_i = jnp.broadcast_to(lam_ref[0, :, n:2 * n], (sub, n))

    def step(j, xr, xi):
        rows = pl.ds(pl.multiple_of(j * sub, sub), sub)
        bur, bui = x_ref[rows, 0:n], x_ref[rows, n:2 * n]
        return rows, lam_r * xr - lam_i * xi + bur, lam_r * xi + lam_i * xr + bui

    def local_end(j, x):
        _, xr, xi = step(j, *x)
        return xr, xi
    zero = jnp.zeros((sub, n), F32)
    end_r, end_i = lax.fori_loop(0, sub_len, local_end, (zero, zero), unroll=2)

    ls_r, ls_i = lams_ref[0, :, 0:n], lams_ref[0, :, n:2 * n]
    cr, ci = carry_ref[:, 0:n], carry_ref[:, n:2 * n]
    init_r, init_i = [], []
    for r in range(sub):
        init_r.append(cr)
        init_i.append(ci)
        cr, ci = (ls_r * cr - ls_i * ci + end_r[r:r + 1], ls_r * ci + ls_i * cr + end_i[r:r + 1])
    carry_ref[:, 0:n] = cr
    carry_ref[:, n:2 * n] = ci

    def scan(j, x):
        rows, xr, xi = step(j, *x)
        x_ref[rows, 0:n] = xr
        x_ref[rows, n:2 * n] = xi
        return xr, xi
    lax.fori_loop(0, sub_len, scan, (jnp.concatenate(init_r, axis=0), jnp.concatenate(init_i, axis=0)), unroll=2)

    y = _dot(x_ref[...].astype(BF16), cmat_ref[0]) + d_ref[...] * perm_ref[...]
    perm_ref[...] = _gelu_tanh(y)

    def permute_out(j, _):
        y_ref[pl.ds(j, sub, stride=sub_len), :] = perm_ref[pl.ds(pl.multiple_of(j * sub, sub), sub), :]
        return 0
    lax.fori_loop(0, sub_len, permute_out, 0)


def s5_scan(u, prep, c_re, c_im, d_skip, chunk):
    l, w = u.shape
    lbr, lbi, lsr, lsi, bbr, bbi = prep
    g = lbr.shape[0]
    nb = g // GROUPS_PER_LANE_BLOCK
    n = STATE_COLS
    bmat = jnp.concatenate([_block_diag(bbr), _block_diag(bbi)], axis=-1).astype(BF16)
    cmat = jnp.concatenate([_block_diag(c_re.transpose(0, 2, 1)),
                            _block_diag(-c_im.transpose(0, 2, 1))], axis=1).astype(BF16)
    lam = jnp.concatenate([lbr.reshape(nb, 1, n), lbi.reshape(nb, 1, n)], axis=-1)
    lams = jnp.concatenate([lsr.reshape(nb, 1, n), lsi.reshape(nb, 1, n)], axis=-1)
    return pl.pallas_call(
        functools.partial(_ssm_kernel, chunk=chunk),
        grid=(nb, l // chunk),
        in_specs=[pl.BlockSpec((chunk, V7X_LANES), lambda b, c: (c, b)),
                  pl.BlockSpec((1, V7X_LANES, 2 * n), lambda b, c: (b, 0, 0)),
                  pl.BlockSpec((1, 2 * n, V7X_LANES), lambda b, c: (b, 0, 0)),
                  pl.BlockSpec((1, 1, 2 * n), lambda b, c: (b, 0, 0)),
                  pl.BlockSpec((1, 1, 2 * n), lambda b, c: (b, 0, 0)),
                  pl.BlockSpec((1, V7X_LANES), lambda b, c: (0, b))],
        out_specs=pl.BlockSpec((chunk, V7X_LANES), lambda b, c: (c, b)),
        out_shape=jax.ShapeDtypeStruct((l, w), F32),
        scratch_shapes=[pltpu.VMEM((chunk, V7X_LANES), F32),
                        pltpu.VMEM((chunk, 2 * n), F32),
                        pltpu.VMEM((1, 2 * n), F32)],
        compiler_params=_params("parallel", "arbitrary"),
        name="s5_scan",
    )(u, bmat, cmat, lam, lams, d_skip.reshape(1, w))


def _mem_kv_kernel(mem_ref, g_ref, w_ref, o_ref):
    o_ref[...] = _dot(_rms(mem_ref[...], g_ref[...]).astype(BF16), w_ref[...]).astype(o_ref.dtype)


def memory_kv(mem, gain, wkv):
    nm, d = mem.shape
    n = wkv.shape[1]
    bn = _pow2_tile(512, n)
    return pl.pallas_call(
        _mem_kv_kernel,
        grid=(n // bn,),
        in_specs=[pl.BlockSpec((nm, d), lambda j: (0, 0)),
                  pl.BlockSpec((1, d), lambda j: (0, 0)),
                  pl.BlockSpec((d, bn), lambda j: (0, j))],
        out_specs=pl.BlockSpec((nm, bn), lambda j: (0, j)),
        out_shape=jax.ShapeDtypeStruct((nm, n), BF16),
        compiler_params=_params("arbitrary"),
        name="memory_kv",
    )(mem, gain.reshape(1, d), wkv)


def _xattn_kernel(xn_ref, h_ref, wq_ref, kv_ref, wo_ref, gpost_ref, gpre_ref, h_out, a_out):
    xw = XA_HEADS * XA_HEAD_DIM
    q = _dot(xn_ref[...], wq_ref[...]).astype(BF16)
    scale = XA_HEAD_DIM ** -0.5
    outs = []
    for hd in range(XA_HEADS):
        cols = slice(hd * XA_HEAD_DIM, (hd + 1) * XA_HEAD_DIM)
        s = _dot_nt(q[:, cols], kv_ref[:, cols]) * scale
        p = jnp.exp(s - jnp.max(s, axis=-1, keepdims=True))
        p = p / jnp.sum(p, axis=-1, keepdims=True)
        outs.append(_dot(p.astype(BF16), kv_ref[:, xw + hd * XA_HEAD_DIM:xw + (hd + 1) * XA_HEAD_DIM]))
    o = jnp.concatenate(outs, axis=1).astype(BF16)
    y = _dot(o, wo_ref[...])
    hn = h_ref[...] + _rms(y, gpost_ref[...])
    h_out[...] = hn
    a_out[...] = _rms(hn, gpre_ref[...]).astype(a_out.dtype)


def cross_attention_block(xn, h, wq, kv, wo, g_post, g_pre):
    m, d = h.shape
    nm, kvw = kv.shape
    xw = wq.shape[1]
    bm = _pow2_tile(256, m)
    row = pl.BlockSpec((bm, d), lambda i: (i, 0))
    gain = pl.BlockSpec((1, d), lambda i: (0, 0))
    return pl.pallas_call(
        _xattn_kernel,
        grid=(m // bm,),
        in_specs=[row, row,
                  pl.BlockSpec((d, xw), lambda i: (0, 0)),
                  pl.BlockSpec((nm, kvw), lambda i: (0, 0)),
                  pl.BlockSpec((xw, d), lambda i: (0, 0)),
                  gain, gain],
        out_specs=[row, row],
        out_shape=[jax.ShapeDtypeStruct((m, d), F32), jax.ShapeDtypeStruct((m, d), BF16)],
        compiler_params=_params("parallel"),
        name="cross_attention",
    )(xn, h, wq, kv, wo, g_post.reshape(1, d), g_pre.reshape(1, d))


def kernel(x, mem, positions, norm_gains, mem_norm_gain, ffn1_w_gu, ffn1_w_down, w_in, attn_sinks, w_attn_out, ssm_lambda_re, ssm_lambda_im, ssm_log_dt, ssm_b_re, ssm_b_im, ssm_c_re, ssm_c_im, ssm_d, w_glu, w_ssm_out, w_o, xa_wq, xa_wkv, xa_wo, ffn2_w_gu, ffn2_w_down):
    batch, l, d = x.shape
    assert batch == 1, "the kernels below walk one sequence"
    depth = norm_gains.shape[0]
    aw = w_attn_out.shape[1]
    sw = w_glu.shape[1]
    n_q = aw // HEAD_DIM
    kw = (w_in.shape[2] - aw - sw - 2 * d) // 2
    n_kv = kw // HEAD_DIM
    c_ssm = aw + 2 * kw
    c_gate = c_ssm + sw
    ssm_chunk = _pow2_tile(1024, l)
    sub_len = ssm_chunk // V7X_SUBLANES

    h = x.reshape(l, d)
    mem2 = mem.reshape(mem.shape[1], d)
    tabs = rope_tables(positions.reshape(l))
    a = rmsnorm_bf16(h, norm_gains[0, 0])
    for i in range(depth):
        g = norm_gains[i]
        bf = lambda w: w[i].astype(BF16)
        t = swiglu_up(a, bf(ffn1_w_gu))
        y = matmul(t, bf(ffn1_w_down), 0, d, F32, "ffn_down")
        h, u = residual_norm(y, h, g[1], g[2], 0.5)
        w_in_i = bf(w_in)
        qkv = matmul(u, w_in_i, 0, c_ssm, F32, "proj_qkv")
        s_in = matmul(u, w_in_i, c_ssm, sw, F32, "proj_ssm")
        gates = matmul(u, w_in_i, c_gate, 2 * d, F32, "proj_gates")
        attn_o = sliding_window_attention(qkv, attn_sinks[i], tabs, n_q, n_kv)
        prep = ssm_prepare(ssm_lambda_re[i], ssm_lambda_im[i], ssm_log_dt[i], ssm_b_re[i], ssm_b_im[i], sub_len)
        y_ssm = s5_scan(s_in, prep, ssm_c_re[i], ssm_c_im[i], ssm_d[i], ssm_chunk)
        glu_o = glu(y_ssm, bf(w_glu))
        mixed = gated_mix(attn_o, glu_o, bf(w_attn_out), bf(w_ssm_out), gates)
        y = matmul(mixed, bf(w_o), 0, d, F32, "mix_out")
        h, xn = residual_norm(y, h, g[3], g[4], 1.0)
        kv = memory_kv(mem2, mem_norm_gain[i], bf(xa_wkv))
        h, a = cross_attention_block(xn, h, bf(xa_wq), kv, bf(xa_wo), g[5], g[6])
        t = swiglu_up(a, bf(ffn2_w_gu))
        y = matmul(t, bf(ffn2_w_down), 0, d, F32, "ffn_down")
        h, a = residual_norm(y, h, g[7], norm_gains[i + 1, 0] if i + 1 < depth else None, 0.5)
    return h.reshape(batch, l, d)
```

```python
import functools
import math

import jax
import jax.numpy as jnp
from jax import lax
from jax.experimental import pallas as pl
from jax.experimental.pallas import tpu as pltpu

F32 = jnp.float32
BF16 = jnp.bfloat16

CHUNK = 64
WINDOW = 128
ATTN_BLOCK = 128
HEAD_DIM = 128
ROT_DIM = HEAD_DIM // 4
ROPE_THETA = 500000.0
SSM_GROUP_CH = 16
SSM_STATE = 64
XA_HEADS = 4
XA_HEAD_DIM = 128
RMS_EPS = 1e-6
MASK_VALUE = -1e30

V7X_LANES = 128
V7X_SUBLANES = 8
V7X_VMEM_LIMIT_BYTES = 52 * 1024 * 1024

GROUPS_PER_LANE_BLOCK = V7X_LANES // SSM_GROUP_CH
STATE_COLS = GROUPS_PER_LANE_BLOCK * SSM_STATE

STREAM_ROWS = 1024
STREAM_COLS = 512
RESIDENT_ROWS = 128
ELEMENTWISE_ROWS = 256
S5_CHUNK = 1024
S5_LANE_BLOCKS = 4


def _pow2_tile(limit, *sizes):
    t = 1
    while t * 2 <= limit and all(s % (t * 2) == 0 for s in sizes):
        t *= 2
    return t


def _params(*sem):
    return pltpu.CompilerParams(dimension_semantics=sem, vmem_limit_bytes=V7X_VMEM_LIMIT_BYTES)


def _rms(x, g):
    ms = jnp.mean(x * x, axis=-1, keepdims=True)
    return x * lax.rsqrt(ms + RMS_EPS) * g


def _dot(a, b):
    return jnp.dot(a, b, preferred_element_type=F32)


def _dot_nt(a, b):
    return lax.dot_general(a, b, (((1,), (1,)), ((), ())), preferred_element_type=F32)


def _layer_cols(layer, k, bn, off=0):
    return pl.BlockSpec((None, k, bn), lambda i, j: (layer, 0, j + off))


def _norm_kernel(x_ref, g_ref, o_ref):
    o_ref[...] = _rms(x_ref[...], g_ref[...]).astype(o_ref.dtype)


def rmsnorm_bf16(x, g):
    m, d = x.shape
    bm = _pow2_tile(ELEMENTWISE_ROWS, m)
    return pl.pallas_call(
        _norm_kernel,
        grid=(m // bm,),
        in_specs=[pl.BlockSpec((bm, d), lambda i: (i, 0)), pl.BlockSpec((1, d), lambda i: (0, 0))],
        out_specs=pl.BlockSpec((bm, d), lambda i: (i, 0)),
        out_shape=jax.ShapeDtypeStruct((m, d), BF16),
        compiler_params=_params("parallel"),
        name="rmsnorm",
    )(x, g.reshape(1, d))


def _mm_kernel(a_ref, w_ref, o_ref):
    o_ref[...] = _dot(a_ref[...], w_ref[...]).astype(o_ref.dtype)


def matmul(a, w, layer, col0, ncols, out_dtype, name):
    m, k = a.shape
    bm = _pow2_tile(STREAM_ROWS, m)
    bn = _pow2_tile(STREAM_COLS, col0, ncols) if col0 else _pow2_tile(STREAM_COLS, ncols)
    return pl.pallas_call(
        _mm_kernel,
        grid=(m // bm, ncols // bn),
        in_specs=[pl.BlockSpec((bm, k), lambda i, j: (i, 0)), _layer_cols(layer, k, bn, col0 // bn)],
        out_specs=pl.BlockSpec((bm, bn), lambda i, j: (i, j)),
        out_shape=jax.ShapeDtypeStruct((m, ncols), out_dtype),
        compiler_params=_params("parallel", "arbitrary"),
        name=name,
    )(a, w)


def _swiglu_kernel(a_ref, wg_ref, wu_ref, o_ref):
    a = a_ref[...]
    gate = _dot(a, wg_ref[...])
    up = _dot(a, wu_ref[...])
    o_ref[...] = (gate * jax.nn.sigmoid(gate) * up).astype(o_ref.dtype)


def swiglu_up(a, w_gu, layer):
    m, k = a.shape
    f = w_gu.shape[2] // 2
    bm = _pow2_tile(STREAM_ROWS, m)
    bn = _pow2_tile(STREAM_COLS, f)
    return pl.pallas_call(
        _swiglu_kernel,
        grid=(m // bm, f // bn),
        in_specs=[pl.BlockSpec((bm, k), lambda i, j: (i, 0)),
                  _layer_cols(layer, k, bn), _layer_cols(layer, k, bn, f // bn)],
        out_specs=pl.BlockSpec((bm, bn), lambda i, j: (i, j)),
        out_shape=jax.ShapeDtypeStruct((m, f), BF16),
        compiler_params=_params("parallel", "arbitrary"),
        name="swiglu_up",
    )(a, w_gu, w_gu)


def _mm_res_norm_kernel(a_ref, w_ref, h_ref, gpost_ref, gpre_ref, h_out, a_out, *, scale):
    hn = h_ref[...] + scale * _rms(_dot(a_ref[...], w_ref[...]), gpost_ref[...])
    h_out[...] = hn
    a_out[...] = _rms(hn, gpre_ref[...]).astype(a_out.dtype)


def _mm_res_kernel(a_ref, w_ref, h_ref, gpost_ref, h_out, *, scale):
    h_out[...] = h_ref[...] + scale * _rms(_dot(a_ref[...], w_ref[...]), gpost_ref[...])


def matmul_residual_norm(a, w, layer, h, g_post, g_pre, scale):
    m, k = a.shape
    d = h.shape[1]
    bm = _pow2_tile(RESIDENT_ROWS, m)
    row = pl.BlockSpec((bm, d), lambda i: (i, 0))
    gain = pl.BlockSpec((1, d), lambda i: (0, 0))
    a_spec = pl.BlockSpec((bm, k), lambda i: (i, 0))
    w_spec = pl.BlockSpec((None, k, d), lambda i: (layer, 0, 0), pipeline_mode=pl.Buffered(1))
    if g_pre is None:
        return pl.pallas_call(
            functools.partial(_mm_res_kernel, scale=scale),
            grid=(m // bm,),
            in_specs=[a_spec, w_spec, row, gain],
            out_specs=row,
            out_shape=jax.ShapeDtypeStruct((m, d), F32),
            compiler_params=_params("parallel"),
            name="matmul_residual",
        )(a, w, h, g_post.reshape(1, d)), None
    return pl.pallas_call(
        functools.partial(_mm_res_norm_kernel, scale=scale),
        grid=(m // bm,),
        in_specs=[a_spec, w_spec, row, gain, gain],
        out_specs=[row, row],
        out_shape=[jax.ShapeDtypeStruct((m, d), F32), jax.ShapeDtypeStruct((m, d), BF16)],
        compiler_params=_params("parallel"),
        name="matmul_residual_norm",
    )(a, w, h, g_post.reshape(1, d), g_pre.reshape(1, d))


def _glu_kernel(y_ref, yj_ref, w_ref, o_ref):
    z = _dot(y_ref[...].astype(BF16), w_ref[...])
    o_ref[...] = (yj_ref[...] * jax.nn.sigmoid(z)).astype(o_ref.dtype)


def glu(y, w, layer):
    m, k = y.shape
    n = w.shape[2]
    bm = _pow2_tile(STREAM_ROWS, m)
    bn = _pow2_tile(STREAM_COLS, n)
    return pl.pallas_call(
        _glu_kernel,
        grid=(m // bm, n // bn),
        in_specs=[pl.BlockSpec((bm, k), lambda i, j: (i, 0)),
                  pl.BlockSpec((bm, bn), lambda i, j: (i, j)),
                  _layer_cols(layer, k, bn)],
        out_specs=pl.BlockSpec((bm, bn), lambda i, j: (i, j)),
        out_shape=jax.ShapeDtypeStruct((m, n), BF16),
        compiler_params=_params("parallel", "arbitrary"),
        name="ssm_glu",
    )(y, y, w)


def _mix_kernel(ao_ref, gl_ref, wa_ref, ws_ref, ga_ref, gs_ref, o_ref):
    y_attn = _dot(ao_ref[...], wa_ref[...])
    y_ssm = _dot(gl_ref[...], ws_ref[...])
    o_ref[...] = (jax.nn.sigmoid(ga_ref[...]) * y_attn
                  + jax.nn.sigmoid(gs_ref[...]) * y_ssm).astype(o_ref.dtype)


def gated_mix(attn_o, glu_o, w_attn_out, w_ssm_out, layer, gates):
    m, ka = attn_o.shape
    ks = glu_o.shape[1]
    d = w_attn_out.shape[2]
    bm = _pow2_tile(STREAM_ROWS, m)
    bn = _pow2_tile(STREAM_COLS, d)
    return pl.pallas_call(
        _mix_kernel,
        grid=(m // bm, d // bn),
        in_specs=[pl.BlockSpec((bm, ka), lambda i, j: (i, 0)),
                  pl.BlockSpec((bm, ks), lambda i, j: (i, 0)),
                  _layer_cols(layer, ka, bn), _layer_cols(layer, ks, bn),
                  pl.BlockSpec((bm, bn), lambda i, j: (i, j)),
                  pl.BlockSpec((bm, bn), lambda i, j: (i, j + d // bn))],
        out_specs=pl.BlockSpec((bm, bn), lambda i, j: (i, j)),
        out_shape=jax.ShapeDtypeStruct((m, d), BF16),
        compiler_params=_params("parallel", "arbitrary"),
        name="gated_mix",
    )(attn_o, glu_o, w_attn_out, w_ssm_out, gates, gates)


def _rope_table_kernel(pos_ref, invf_ref, a_ref, b_ref, c_ref):
    ang = pos_ref[...].astype(F32) * invf_ref[...]
    cos, sin = jnp.cos(ang), jnp.sin(ang)
    lane = lax.broadcasted_iota(jnp.int32, ang.shape, 1)
    half = ROT_DIM // 2
    a_ref[...] = jnp.where(lane < ROT_DIM, cos, 1.0)
    b_ref[...] = jnp.where(lane < half, -sin, 0.0)
    c_ref[...] = jnp.where((lane >= half) & (lane < ROT_DIM), sin, 0.0)


def rope_tables(positions):
    l = positions.shape[0]
    inv_freq = ROPE_THETA ** (-jnp.arange(0, ROT_DIM, 2, dtype=F32) / ROT_DIM)
    invf = jnp.concatenate([inv_freq, inv_freq, jnp.zeros((HEAD_DIM - ROT_DIM,), F32)]).reshape(1, HEAD_DIM)
    bm = _pow2_tile(STREAM_ROWS, l)
    tab = pl.BlockSpec((bm, HEAD_DIM), lambda i: (i, 0))
    shape = jax.ShapeDtypeStruct((l, HEAD_DIM), F32)
    return pl.pallas_call(
        _rope_table_kernel,
        grid=(l // bm,),
        in_specs=[pl.BlockSpec((bm, 1), lambda i: (i, 0)), pl.BlockSpec((1, HEAD_DIM), lambda i: (0, 0))],
        out_specs=[tab, tab, tab],
        out_shape=[shape, shape, shape],
        compiler_params=_params("parallel"),
        name="rope_tables",
    )(positions.reshape(l, 1), invf)


def _rope(x, a, b, c):
    half = ROT_DIM // 2
    return x * a + pltpu.roll(x, HEAD_DIM - half, 1) * b + pltpu.roll(x, half, 1) * c


def _swa_kernel(sink_ref, q_ref, kp_ref, kc_ref, vp_ref, vc_ref, ap_ref, bp_ref, cp_ref,
                ac_ref, bc_ref, cc_ref, o_ref, *, n_kv, group):
    i = pl.program_id(0)
    blk = ATTN_BLOCK
    rows = group * blk
    ac, bc, cc = ac_ref[...], bc_ref[...], cc_ref[...]
    ap, bp, cp = ap_ref[...], bp_ref[...], cp_ref[...]

    q_idx = lax.broadcasted_iota(jnp.int32, (rows, 2 * blk), 0) & (blk - 1)
    k_idx = lax.broadcasted_iota(jnp.int32, (rows, 2 * blk), 1)
    chunk_shift = CHUNK.bit_length() - 1
    q_chunk = q_idx >> chunk_shift
    k_chunk = k_idx >> chunk_shift
    cpb = blk // CHUNK
    win = WINDOW // CHUNK
    band = (k_chunk >= q_chunk + cpb - win) & (k_chunk <= q_chunk + cpb)
    first_valid_key = jnp.where(i > 0, 0, blk)
    mask = band & (k_idx >= first_valid_key)
    scale = HEAD_DIM ** -0.5

    for kh in range(n_kv):
        ks = slice(kh * HEAD_DIM, (kh + 1) * HEAD_DIM)
        k = jnp.concatenate([_rope(kp_ref[:, ks], ap, bp, cp), _rope(kc_ref[:, ks], ac, bc, cc)], axis=0)
        v = jnp.concatenate([vp_ref[:, ks], vc_ref[:, ks]], axis=0).astype(BF16)
        heads = [kh * group + g for g in range(group)]
        q = jnp.concatenate([_rope(q_ref[:, h * HEAD_DIM:(h + 1) * HEAD_DIM], ac, bc, cc) for h in heads], axis=0)
        sink = jnp.concatenate([jnp.full((blk, 1), sink_ref[0, h], F32) for h in heads], axis=0)
        s = _dot_nt(q.astype(BF16), k.astype(BF16)) * scale
        s = jnp.where(mask, s, MASK_VALUE)
        m = jnp.maximum(jnp.max(s, axis=-1, keepdims=True), sink)
        p = jnp.exp(s - m)
        den = jnp.sum(p, axis=-1, keepdims=True) + jnp.exp(sink - m)
        o = _dot((p / den).astype(BF16), v)
        for g, h in enumerate(heads):
            o_ref[:, h * HEAD_DIM:(h + 1) * HEAD_DIM] = o[g * blk:(g + 1) * blk].astype(o_ref.dtype)


def sliding_window_attention(qkv, sinks, tabs, n_q, n_kv):
    l = qkv.shape[0]
    aw, kw = n_q * HEAD_DIM, n_kv * HEAD_DIM
    blk = ATTN_BLOCK
    k_col, v_col = aw // kw, aw // kw + 1
    assert aw % kw == 0
    prev = lambda i: jnp.maximum(i - 1, 0)
    tab_c = pl.BlockSpec((blk, HEAD_DIM), lambda i: (i, 0))
    tab_p = pl.BlockSpec((blk, HEAD_DIM), lambda i: (prev(i), 0))
    ta, tb, tc = tabs
    return pl.pallas_call(
        functools.partial(_swa_kernel, n_kv=n_kv, group=n_q // n_kv),
        grid=(l // blk,),
        in_specs=[pl.BlockSpec(memory_space=pltpu.SMEM),
                  pl.BlockSpec((blk, aw), lambda i: (i, 0)),
                  pl.BlockSpec((blk, kw), lambda i: (prev(i), k_col)),
                  pl.BlockSpec((blk, kw), lambda i: (i, k_col)),
                  pl.BlockSpec((blk, kw), lambda i: (prev(i), v_col)),
                  pl.BlockSpec((blk, kw), lambda i: (i, v_col)),
                  tab_p, tab_p, tab_p, tab_c, tab_c, tab_c],
        out_specs=pl.BlockSpec((blk, aw), lambda i: (i, 0)),
        out_shape=jax.ShapeDtypeStruct((l, aw), BF16),
        compiler_params=_params("parallel"),
        name="swa_attention",
    )(sinks.reshape(1, n_q), qkv, qkv, qkv, qkv, qkv, ta, tb, tc, ta, tb, tc)


def _ssm_prep_kernel(lre_ref, lim_ref, ldt_ref, bre_ref, bim_ref,
                     lbr_ref, lbi_ref, lsr_ref, lsi_ref, bbr_ref, bbi_ref, *, sub_len):
    lr = jnp.minimum(lre_ref[...], -1e-4)
    li = lim_ref[...]
    dt = jnp.exp(ldt_ref[...])
    ar, ai = lr * dt, li * dt
    mag = jnp.exp(ar)
    lbr, lbi = mag * jnp.cos(ai), mag * jnp.sin(ai)
    lbr_ref[...] = lbr
    lbi_ref[...] = lbi
    mag_s = jnp.exp(sub_len * ar)
    lsr_ref[...] = mag_s * jnp.cos(sub_len * ai)
    lsi_ref[...] = mag_s * jnp.sin(sub_len * ai)
    nr, ni = lbr - 1.0, lbi
    den = lr * lr + li * li
    cr = (nr * lr + ni * li) / den
    ci = (ni * lr - nr * li) / den
    bre, bim = bre_ref[...], bim_ref[...]
    bbr_ref[...] = cr * bre - ci * bim
    bbi_ref[...] = cr * bim + ci * bre


def ssm_prepare(lam_re, lam_im, log_dt, b_re, b_im, sub_len):
    g, p = lam_re.shape
    h = b_re.shape[-1]
    gp = jax.ShapeDtypeStruct((g, 1, p), F32)
    ghp = jax.ShapeDtypeStruct((g, h, p), F32)
    return pl.pallas_call(
        functools.partial(_ssm_prep_kernel, sub_len=float(sub_len)),
        out_shape=[gp, gp, gp, gp, ghp, ghp],
        name="ssm_prepare",
    )(lam_re.reshape(g, 1, p), lam_im.reshape(g, 1, p), log_dt.reshape(g, 1, 1),
      b_re.transpose(0, 2, 1), b_im.transpose(0, 2, 1))


def _block_diag(w):
    g, a, b = w.shape
    n = GROUPS_PER_LANE_BLOCK
    w4 = w.reshape(g // n, n, a, b)
    eye = jnp.eye(n, dtype=w.dtype)
    return (w4[:, :, :, None, :] * eye[None, :, None, :, None]).reshape(g // n, n * a, n * b)


def _gelu_tanh(x):
    return 0.5 * x * (1.0 + jnp.tanh(math.sqrt(2.0 / math.pi) * (x + 0.044715 * (x * x * x))))


def _ssm_kernel(u_ref, bmat_ref, cmat_ref, lam_ref, lams_ref, d_ref, y_ref,
                perm_ref, x_ref, carry_ref, *, chunk, lane_blocks):
    sub = V7X_SUBLANES
    sub_len = chunk // sub
    n = STATE_COLS

    @pl.when(pl.program_id(1) == 0)
    def _():
        carry_ref[...] = jnp.zeros_like(carry_ref)

    for lb in range(lane_blocks):
        lanes = slice(lb * V7X_LANES, (lb + 1) * V7X_LANES)
        perm_ref[...] = pltpu.einshape(
            "rjl->jrl", u_ref[:, lanes].reshape(sub, sub_len, V7X_LANES)).reshape(chunk, V7X_LANES)
        x_ref[...] = _dot(perm_ref[...].astype(BF16), bmat_ref[lb])

        lam_r = jnp.broadcast_to(lam_ref[lb, :, 0:n], (sub, n))
        lam_i = jnp.broadcast_to(lam_ref[lb, :, n:2 * n], (sub, n))

        def step(j, xr, xi):
            rows = pl.ds(pl.multiple_of(j * sub, sub), sub)
            bur, bui = x_ref[rows, 0:n], x_ref[rows, n:2 * n]
            return rows, lam_r * xr - lam_i * xi + bur, lam_r * xi + lam_i * xr + bui

        def local_end(j, x):
            _, xr, xi = step(j, *x)
            return xr, xi
        zero = jnp.zeros((sub, n), F32)
        end_r, end_i = lax.fori_loop(0, sub_len, local_end, (zero, zero), unroll=2)

        ls_r, ls_i = lams_ref[lb, :, 0:n], lams_ref[lb, :, n:2 * n]
        cr, ci = carry_ref[lb, :, 0:n], carry_ref[lb, :, n:2 * n]
        init_r, init_i = [], []
        for r in range(sub):
            init_r.append(cr)
            init_i.append(ci)
            cr, ci = (ls_r * cr - ls_i * ci + end_r[r:r + 1], ls_r * ci + ls_i * cr + end_i[r:r + 1])
        carry_ref[lb, :, 0:n] = cr
        carry_ref[lb, :, n:2 * n] = ci

        def scan(j, x):
            rows, xr, xi = step(j, *x)
            x_ref[rows, 0:n] = xr
            x_ref[rows, n:2 * n] = xi
            return xr, xi
        lax.fori_loop(0, sub_len, scan,
                      (jnp.concatenate(init_r, axis=0), jnp.concatenate(init_i, axis=0)), unroll=2)

        y = _dot(x_ref[...].astype(BF16), cmat_ref[lb]) + d_ref[:, lanes] * perm_ref[...]
        y_ref[:, lanes] = pltpu.einshape(
            "jrl->rjl", _gelu_tanh(y).reshape(sub_len, sub, V7X_LANES)).reshape(chunk, V7X_LANES)


def s5_scan(u, prep, c_re, c_im, d_skip, chunk):
    l, w = u.shape
    lbr, lbi, lsr, lsi, bbr, bbi = prep
    g = lbr.shape[0]
    nb = g // GROUPS_PER_LANE_BLOCK
    n = STATE_COLS
    lane_blocks = _pow2_tile(S5_LANE_BLOCKS, nb)
    wide = lane_blocks * V7X_LANES
    bmat = jnp.concatenate([_block_diag(bbr), _block_diag(bbi)], axis=-1).astype(BF16)
    cmat = jnp.concatenate([_block_diag(c_re.transpose(0, 2, 1)),
                            _block_diag(-c_im.transpose(0, 2, 1))], axis=1).astype(BF16)
    lam = jnp.concatenate([lbr.reshape(nb, 1, n), lbi.reshape(nb, 1, n)], axis=-1)
    lams = jnp.concatenate([lsr.reshape(nb, 1, n), lsi.reshape(nb, 1, n)], axis=-1)
    per_block = lambda *tail: pl.BlockSpec((lane_blocks,) + tail, lambda b, c: (b, 0, 0))
    return pl.pallas_call(
        functools.partial(_ssm_kernel, chunk=chunk, lane_blocks=lane_blocks),
        grid=(nb // lane_blocks, l // chunk),
        in_specs=[pl.BlockSpec((chunk, wide), lambda b, c: (c, b)),
                  per_block(V7X_LANES, 2 * n), per_block(2 * n, V7X_LANES),
                  per_block(1, 2 * n), per_block(1, 2 * n),
                  pl.BlockSpec((1, wide), lambda b, c: (0, b))],
        out_specs=pl.BlockSpec((chunk, wide), lambda b, c: (c, b)),
        out_shape=jax.ShapeDtypeStruct((l, w), F32),
        scratch_shapes=[pltpu.VMEM((chunk, V7X_LANES), F32),
                        pltpu.VMEM((chunk, 2 * n), F32),
                        pltpu.VMEM((lane_blocks, 1, 2 * n), F32)],
        compiler_params=_params("parallel", "arbitrary"),
        name="s5_scan",
    )(u, bmat, cmat, lam, lams, d_skip.reshape(1, w))


def _mem_kv_kernel(mem_ref, g_ref, w_ref, o_ref):
    o_ref[...] = _dot(_rms(mem_ref[...], g_ref[...]).astype(BF16), w_ref[...]).astype(o_ref.dtype)


def memory_kv(mem, gain, wkv, layer):
    nm, d = mem.shape
    n = wkv.shape[2]
    bn = _pow2_tile(STREAM_COLS, n)
    return pl.pallas_call(
        _mem_kv_kernel,
        grid=(n // bn,),
        in_specs=[pl.BlockSpec((nm, d), lambda j: (0, 0)),
                  pl.BlockSpec((1, d), lambda j: (0, 0)),
                  pl.BlockSpec((None, d, bn), lambda j: (layer, 0, j))],
        out_specs=pl.BlockSpec((nm, bn), lambda j: (0, j)),
        out_shape=jax.ShapeDtypeStruct((nm, n), BF16),
        compiler_params=_params("arbitrary"),
        name="memory_kv",
    )(mem, gain.reshape(1, d), wkv)


def _xattn_kernel(xn_ref, h_ref, wq_ref, kv_ref, wo_ref, gpost_ref, gpre_ref, h_out, a_out):
    xw = XA_HEADS * XA_HEAD_DIM
    q = _dot(xn_ref[...], wq_ref[...]).astype(BF16)
    scale = XA_HEAD_DIM ** -0.5
    outs = []
    for hd in range(XA_HEADS):
        cols = slice(hd * XA_HEAD_DIM, (hd + 1) * XA_HEAD_DIM)
        s = _dot_nt(q[:, cols], kv_ref[:, cols]) * scale
        p = jnp.exp(s - jnp.max(s, axis=-1, keepdims=True))
        p = p / jnp.sum(p, axis=-1, keepdims=True)
        outs.append(_dot(p.astype(BF16), kv_ref[:, xw + hd * XA_HEAD_DIM:xw + (hd + 1) * XA_HEAD_DIM]))
    o = jnp.concatenate(outs, axis=1).astype(BF16)
    y = _dot(o, wo_ref[...])
    hn = h_ref[...] + _rms(y, gpost_ref[...])
    h_out[...] = hn
    a_out[...] = _rms(hn, gpre_ref[...]).astype(a_out.dtype)


def cross_attention_block(xn, h, wq, kv, wo, layer, g_post, g_pre):
    m, d = h.shape
    nm, kvw = kv.shape
    xw = wq.shape[2]
    bm = _pow2_tile(ELEMENTWISE_ROWS, m)
    row = pl.BlockSpec((bm, d), lambda i: (i, 0))
    gain = pl.BlockSpec((1, d), lambda i: (0, 0))
    return pl.pallas_call(
        _xattn_kernel,
        grid=(m // bm,),
        in_specs=[row, row,
                  pl.BlockSpec((None, d, xw), lambda i: (layer, 0, 0)),
                  pl.BlockSpec((nm, kvw), lambda i: (0, 0)),
                  pl.BlockSpec((None, xw, d), lambda i: (layer, 0, 0)),
                  gain, gain],
        out_specs=[row, row],
        out_shape=[jax.ShapeDtypeStruct((m, d), F32), jax.ShapeDtypeStruct((m, d), BF16)],
        compiler_params=_params("parallel"),
        name="cross_attention",
    )(xn, h, wq, kv, wo, g_post.reshape(1, d), g_pre.reshape(1, d))


def kernel(x, mem, positions, norm_gains, mem_norm_gain, ffn1_w_gu, ffn1_w_down, w_in, attn_sinks, w_attn_out, ssm_lambda_re, ssm_lambda_im, ssm_log_dt, ssm_b_re, ssm_b_im, ssm_c_re, ssm_c_im, ssm_d, w_glu, w_ssm_out, w_o, xa_wq, xa_wkv, xa_wo, ffn2_w_gu, ffn2_w_down):
    batch, l, d = x.shape
    assert batch == 1, "the kernels below walk one sequence"
    depth = norm_gains.shape[0]
    aw = w_attn_out.shape[1]
    sw = w_glu.shape[1]
    n_q = aw // HEAD_DIM
    kw = (w_in.shape[2] - aw - sw - 2 * d) // 2
    n_kv = kw // HEAD_DIM
    c_ssm = aw + 2 * kw
    c_gate = c_ssm + sw
    ssm_chunk = _pow2_tile(S5_CHUNK, l)
    sub_len = ssm_chunk // V7X_SUBLANES

    bf = lambda w: w.astype(BF16)
    ffn1_w_gu, ffn1_w_down, ffn2_w_gu, ffn2_w_down = bf(ffn1_w_gu), bf(ffn1_w_down), bf(ffn2_w_gu), bf(ffn2_w_down)
    w_in, w_attn_out, w_glu, w_ssm_out, w_o = bf(w_in), bf(w_attn_out), bf(w_glu), bf(w_ssm_out), bf(w_o)
    xa_wq, xa_wkv, xa_wo = bf(xa_wq), bf(xa_wkv), bf(xa_wo)

    h = x.reshape(l, d)
    mem2 = mem.reshape(mem.shape[1], d)
    tabs = rope_tables(positions.reshape(l))
    a = rmsnorm_bf16(h, norm_gains[0, 0])
    for i in range(depth):
        g = norm_gains[i]
        t = swiglu_up(a, ffn1_w_gu, i)
        h, u = matmul_residual_norm(t, ffn1_w_down, i, h, g[1], g[2], 0.5)
        qkv = matmul(u, w_in, i, 0, c_ssm, F32, "proj_qkv")
        s_in = matmul(u, w_in, i, c_ssm, sw, F32, "proj_ssm")
        gates = matmul(u, w_in, i, c_gate, 2 * d, F32, "proj_gates")
        attn_o = sliding_window_attention(qkv, attn_sinks[i], tabs, n_q, n_kv)
        prep = ssm_prepare(ssm_lambda_re[i], ssm_lambda_im[i], ssm_log_dt[i], ssm_b_re[i], ssm_b_im[i], sub_len)
        y_ssm = s5_scan(s_in, prep, ssm_c_re[i], ssm_c_im[i], ssm_d[i], ssm_chunk)
        glu_o = glu(y_ssm, w_glu, i)
        mixed = gated_mix(attn_o, glu_o, w_attn_out, w_ssm_out, i, gates)
        h, xn = matmul_residual_norm(mixed, w_o, i, h, g[3], g[4], 1.0)
        kv = memory_kv(mem2, mem_norm_gain[i], xa_wkv, i)
        h, a = cross_attention_block(xn, h, xa_wq, kv, xa_wo, i, g[5], g[6])
        t = swiglu_up(a, ffn2_w_gu, i)
        h, a = matmul_residual_norm(t, ffn2_w_down, i, h, g[7], norm_gains[i + 1, 0] if i + 1 < depth else None, 0.5)
    return h.reshape(batch, l, d)
```

```python
import functools
import math

import jax
import jax.numpy as jnp
from jax import lax
from jax.experimental import pallas as pl
from jax.experimental.pallas import tpu as pltpu

F32 = jnp.float32
BF16 = jnp.bfloat16

CHUNK = 64
WINDOW = 128
ATTN_BLOCK = 128
HEAD_DIM = 128
ROT_DIM = HEAD_DIM // 4
ROPE_THETA = 500000.0
SSM_GROUP_CH = 16
SSM_STATE = 64
XA_HEADS = 4
XA_HEAD_DIM = 128
RMS_EPS = 1e-6
MASK_VALUE = -1e30

V7X_LANES = 128
V7X_SUBLANES = 8
V7X_VMEM_LIMIT_BYTES = 52 * 1024 * 1024

GROUPS_PER_LANE_BLOCK = V7X_LANES // SSM_GROUP_CH
STATE_COLS = GROUPS_PER_LANE_BLOCK * SSM_STATE

STREAM_ROWS = 1024
STREAM_COLS = 512
RESIDENT_ROWS = 128
ELEMENTWISE_ROWS = 256
S5_CHUNK = 1024
S5_LANE_BLOCKS = 4


def _pow2_tile(limit, *sizes):
    t = 1
    while t * 2 <= limit and all(s % (t * 2) == 0 for s in sizes):
        t *= 2
    return t


def _params(*sem):
    return pltpu.CompilerParams(dimension_semantics=sem, vmem_limit_bytes=V7X_VMEM_LIMIT_BYTES)


def _rms(x, g):
    ms = jnp.mean(x * x, axis=-1, keepdims=True)
    return x * lax.rsqrt(ms + RMS_EPS) * g


def _dot(a, b):
    return jnp.dot(a, b, preferred_element_type=F32)


def _dot_nt(a, b):
    return lax.dot_general(a, b, (((1,), (1,)), ((), ())), preferred_element_type=F32)


def _layer_cols(layer, k, bn, off=0):
    return pl.BlockSpec((None, k, bn), lambda i, j: (layer, 0, j + off))


def _norm_kernel(x_ref, g_ref, o_ref):
    o_ref[...] = _rms(x_ref[...], g_ref[...]).astype(o_ref.dtype)


def rmsnorm_bf16(x, g):
    m, d = x.shape
    bm = _pow2_tile(ELEMENTWISE_ROWS, m)
    return pl.pallas_call(
        _norm_kernel,
        grid=(m // bm,),
        in_specs=[pl.BlockSpec((bm, d), lambda i: (i, 0)), pl.BlockSpec((1, d), lambda i: (0, 0))],
        out_specs=pl.BlockSpec((bm, d), lambda i: (i, 0)),
        out_shape=jax.ShapeDtypeStruct((m, d), BF16),
        compiler_params=_params("parallel"),
        name="rmsnorm",
    )(x, g.reshape(1, d))


def _mm_kernel(a_ref, w_ref, o_ref):
    o_ref[...] = _dot(a_ref[...], w_ref[...]).astype(o_ref.dtype)


def matmul(a, w, layer, col0, ncols, out_dtype, name):
    m, k = a.shape
    bm = _pow2_tile(STREAM_ROWS, m)
    bn = _pow2_tile(STREAM_COLS, col0, ncols) if col0 else _pow2_tile(STREAM_COLS, ncols)
    return pl.pallas_call(
        _mm_kernel,
        grid=(m // bm, ncols // bn),
        in_specs=[pl.BlockSpec((bm, k), lambda i, j: (i, 0)), _layer_cols(layer, k, bn, col0 // bn)],
        out_specs=pl.BlockSpec((bm, bn), lambda i, j: (i, j)),
        out_shape=jax.ShapeDtypeStruct((m, ncols), out_dtype),
        compiler_params=_params("parallel", "arbitrary"),
        name=name,
    )(a, w)


def _swiglu_kernel(a_ref, wg_ref, wu_ref, o_ref):
    a = a_ref[...]
    gate = _dot(a, wg_ref[...])
    up = _dot(a, wu_ref[...])
    o_ref[...] = (gate * jax.nn.sigmoid(gate) * up).astype(o_ref.dtype)


def swiglu_up(a, w_gu, layer):
    m, k = a.shape
    f = w_gu.shape[2] // 2
    bm = _pow2_tile(STREAM_ROWS, m)
    bn = _pow2_tile(STREAM_COLS, f)
    return pl.pallas_call(
        _swiglu_kernel,
        grid=(m // bm, f // bn),
        in_specs=[pl.BlockSpec((bm, k), lambda i, j: (i, 0)),
                  _layer_cols(layer, k, bn), _layer_cols(layer, k, bn, f // bn)],
        out_specs=pl.BlockSpec((bm, bn), lambda i, j: (i, j)),
        out_shape=jax.ShapeDtypeStruct((m, f), BF16),
        compiler_params=_params("parallel", "arbitrary"),
        name="swiglu_up",
    )(a, w_gu, w_gu)


def _mm_res_norm_kernel(a_ref, w_ref, h_ref, gpost_ref, *rest, scale, with_norm):
    if with_norm:
        gpre_ref, h_out, a_out, y_even, y_odd = rest
    else:
        h_out, y_even, y_odd = rest
    s = pl.program_id(0)

    @pl.when(s == 0)
    def _():
        y_odd[...] = jnp.zeros_like(y_odd)

    def phase(y_new, y_old):
        y_new[...] = _dot(a_ref[...], w_ref[...])
        hn = h_ref[...] + scale * _rms(y_old[...], gpost_ref[...])
        h_out[...] = hn
        if with_norm:
            a_out[...] = _rms(hn, gpre_ref[...]).astype(a_out.dtype)

    @pl.when(s % 2 == 0)
    def _():
        phase(y_even, y_odd)

    @pl.when(s % 2 == 1)
    def _():
        phase(y_odd, y_even)


def matmul_residual_norm(a, w, layer, h, g_post, g_pre, scale):
    m, k = a.shape
    d = h.shape[1]
    bm = _pow2_tile(RESIDENT_ROWS, m)
    n_blocks = m // bm
    done = lambda s: (jnp.maximum(s - 1, 0), 0)
    row = pl.BlockSpec((bm, d), done)
    gain = pl.BlockSpec((1, d), lambda s: (0, 0))
    a_spec = pl.BlockSpec((bm, k), lambda s: (jnp.minimum(s, n_blocks - 1), 0))
    w_spec = pl.BlockSpec((None, k, d), lambda s: (layer, 0, 0), pipeline_mode=pl.Buffered(1))
    with_norm = g_pre is not None
    gains = [g_post.reshape(1, d)] + ([g_pre.reshape(1, d)] if with_norm else [])
    h_shape = jax.ShapeDtypeStruct((m, d), F32)
    out = pl.pallas_call(
        functools.partial(_mm_res_norm_kernel, scale=scale, with_norm=with_norm),
        grid=(n_blocks + 1,),
        in_specs=[a_spec, w_spec, row] + [gain] * len(gains),
        out_specs=[row, row] if with_norm else row,
        out_shape=[h_shape, jax.ShapeDtypeStruct((m, d), BF16)] if with_norm else h_shape,
        scratch_shapes=[pltpu.VMEM((bm, d), F32), pltpu.VMEM((bm, d), F32)],
        compiler_params=_params("arbitrary"),
        name="matmul_residual_norm",
    )(a, w, h, *gains)
    return out if with_norm else (out, None)


def _glu_kernel(y_ref, yj_ref, w_ref, o_ref):
    z = _dot(y_ref[...].astype(BF16), w_ref[...])
    o_ref[...] = (yj_ref[...] * jax.nn.sigmoid(z)).astype(o_ref.dtype)


def glu(y, w, layer):
    m, k = y.shape
    n = w.shape[2]
    bm = _pow2_tile(STREAM_ROWS, m)
    bn = _pow2_tile(STREAM_COLS, n)
    return pl.pallas_call(
        _glu_kernel,
        grid=(m // bm, n // bn),
        in_specs=[pl.BlockSpec((bm, k), lambda i, j: (i, 0)),
                  pl.BlockSpec((bm, bn), lambda i, j: (i, j)),
                  _layer_cols(layer, k, bn)],
        out_specs=pl.BlockSpec((bm, bn), lambda i, j: (i, j)),
        out_shape=jax.ShapeDtypeStruct((m, n), BF16),
        compiler_params=_params("parallel", "arbitrary"),
        name="ssm_glu",
    )(y, y, w)


def _mix_kernel(ao_ref, gl_ref, wa_ref, ws_ref, ga_ref, gs_ref, o_ref):
    y_attn = _dot(ao_ref[...], wa_ref[...])
    y_ssm = _dot(gl_ref[...], ws_ref[...])
    o_ref[...] = (jax.nn.sigmoid(ga_ref[...]) * y_attn
                  + jax.nn.sigmoid(gs_ref[...]) * y_ssm).astype(o_ref.dtype)


def gated_mix(attn_o, glu_o, w_attn_out, w_ssm_out, layer, gates):
    m, ka = attn_o.shape
    ks = glu_o.shape[1]
    d = w_attn_out.shape[2]
    bm = _pow2_tile(STREAM_ROWS, m)
    bn = _pow2_tile(STREAM_COLS, d)
    return pl.pallas_call(
        _mix_kernel,
        grid=(m // bm, d // bn),
        in_specs=[pl.BlockSpec((bm, ka), lambda i, j: (i, 0)),
                  pl.BlockSpec((bm, ks), lambda i, j: (i, 0)),
                  _layer_cols(layer, ka, bn), _layer_cols(layer, ks, bn),
                  pl.BlockSpec((bm, bn), lambda i, j: (i, j)),
                  pl.BlockSpec((bm, bn), lambda i, j: (i, j + d // bn))],
        out_specs=pl.BlockSpec((bm, bn), lambda i, j: (i, j)),
        out_shape=jax.ShapeDtypeStruct((m, d), BF16),
        compiler_params=_params("parallel", "arbitrary"),
        name="gated_mix",
    )(attn_o, glu_o, w_attn_out, w_ssm_out, gates, gates)


def _rope_table_kernel(pos_ref, invf_ref, a_ref, b_ref, c_ref):
    ang = pos_ref[...].astype(F32) * invf_ref[...]
    cos, sin = jnp.cos(ang), jnp.sin(ang)
    lane = lax.broadcasted_iota(jnp.int32, ang.shape, 1)
    half = ROT_DIM // 2
    a_ref[...] = jnp.where(lane < ROT_DIM, cos, 1.0)
    b_ref[...] = jnp.where(lane < half, -sin, 0.0)
    c_ref[...] = jnp.where((lane >= half) & (lane < ROT_DIM), sin, 0.0)


def rope_tables(positions):
    l = positions.shape[0]
    inv_freq = ROPE_THETA ** (-jnp.arange(0, ROT_DIM, 2, dtype=F32) / ROT_DIM)
    invf = jnp.concatenate([inv_freq, inv_freq, jnp.zeros((HEAD_DIM - ROT_DIM,), F32)]).reshape(1, HEAD_DIM)
    bm = _pow2_tile(STREAM_ROWS, l)
    tab = pl.BlockSpec((bm, HEAD_DIM), lambda i: (i, 0))
    shape = jax.ShapeDtypeStruct((l, HEAD_DIM), F32)
    return pl.pallas_call(
        _rope_table_kernel,
        grid=(l // bm,),
        in_specs=[pl.BlockSpec((bm, 1), lambda i: (i, 0)), pl.BlockSpec((1, HEAD_DIM), lambda i: (0, 0))],
        out_specs=[tab, tab, tab],
        out_shape=[shape, shape, shape],
        compiler_params=_params("parallel"),
        name="rope_tables",
    )(positions.reshape(l, 1), invf)


def _rope(x, a, b, c):
    half = ROT_DIM // 2
    return x * a + pltpu.roll(x, HEAD_DIM - half, 1) * b + pltpu.roll(x, half, 1) * c


def _swa_kernel(sink_ref, q_ref, kp_ref, kc_ref, vp_ref, vc_ref, ap_ref, bp_ref, cp_ref,
                ac_ref, bc_ref, cc_ref, o_ref, *, n_kv, group):
    i = pl.program_id(0)
    blk = ATTN_BLOCK
    rows = group * blk
    ac, bc, cc = ac_ref[...], bc_ref[...], cc_ref[...]
    ap, bp, cp = ap_ref[...], bp_ref[...], cp_ref[...]

    q_idx = lax.broadcasted_iota(jnp.int32, (rows, 2 * blk), 0) & (blk - 1)
    k_idx = lax.broadcasted_iota(jnp.int32, (rows, 2 * blk), 1)
    chunk_shift = CHUNK.bit_length() - 1
    q_chunk = q_idx >> chunk_shift
    k_chunk = k_idx >> chunk_shift
    cpb = blk // CHUNK
    win = WINDOW // CHUNK
    band = (k_chunk >= q_chunk + cpb - win) & (k_chunk <= q_chunk + cpb)
    first_valid_key = jnp.where(i > 0, 0, blk)
    mask = band & (k_idx >= first_valid_key)
    scale = HEAD_DIM ** -0.5

    for kh in range(n_kv):
        ks = slice(kh * HEAD_DIM, (kh + 1) * HEAD_DIM)
        k = jnp.concatenate([_rope(kp_ref[:, ks], ap, bp, cp), _rope(kc_ref[:, ks], ac, bc, cc)], axis=0)
        v = jnp.concatenate([vp_ref[:, ks], vc_ref[:, ks]], axis=0).astype(BF16)
        heads = [kh * group + g for g in range(group)]
        q = jnp.concatenate([_rope(q_ref[:, h * HEAD_DIM:(h + 1) * HEAD_DIM], ac, bc, cc) for h in heads], axis=0)
        sink = jnp.concatenate([jnp.full((blk, 1), sink_ref[0, h], F32) for h in heads], axis=0)
        s = _dot_nt(q.astype(BF16), k.astype(BF16)) * scale
        s = jnp.where(mask, s, MASK_VALUE)
        m = jnp.maximum(jnp.max(s, axis=-1, keepdims=True), sink)
        p = jnp.exp(s - m)
        den = jnp.sum(p, axis=-1, keepdims=True) + jnp.exp(sink - m)
        o = _dot((p / den).astype(BF16), v)
        for g, h in enumerate(heads):
            o_ref[:, h * HEAD_DIM:(h + 1) * HEAD_DIM] = o[g * blk:(g + 1) * blk].astype(o_ref.dtype)


def sliding_window_attention(qkv, sinks, tabs, n_q, n_kv):
    l = qkv.shape[0]
    aw, kw = n_q * HEAD_DIM, n_kv * HEAD_DIM
    blk = ATTN_BLOCK
    k_col, v_col = aw // kw, aw // kw + 1
    assert aw % kw == 0
    prev = lambda i: jnp.maximum(i - 1, 0)
    tab_c = pl.BlockSpec((blk, HEAD_DIM), lambda i: (i, 0))
    tab_p = pl.BlockSpec((blk, HEAD_DIM), lambda i: (prev(i), 0))
    ta, tb, tc = tabs
    return pl.pallas_call(
        functools.partial(_swa_kernel, n_kv=n_kv, group=n_q // n_kv),
        grid=(l // blk,),
        in_specs=[pl.BlockSpec(memory_space=pltpu.SMEM),
                  pl.BlockSpec((blk, aw), lambda i: (i, 0)),
                  pl.BlockSpec((blk, kw), lambda i: (prev(i), k_col)),
                  pl.BlockSpec((blk, kw), lambda i: (i, k_col)),
                  pl.BlockSpec((blk, kw), lambda i: (prev(i), v_col)),
                  pl.BlockSpec((blk, kw), lambda i: (i, v_col)),
                  tab_p, tab_p, tab_p, tab_c, tab_c, tab_c],
        out_specs=pl.BlockSpec((blk, aw), lambda i: (i, 0)),
        out_shape=jax.ShapeDtypeStruct((l, aw), BF16),
        compiler_params=_params("parallel"),
        name="swa_attention",
    )(sinks.reshape(1, n_q), qkv, qkv, qkv, qkv, qkv, ta, tb, tc, ta, tb, tc)


def _ssm_prep_kernel(lre_ref, lim_ref, ldt_ref, bre_ref, bim_ref,
                     lbr_ref, lbi_ref, lsr_ref, lsi_ref, bbr_ref, bbi_ref, *, sub_len):
    lr = jnp.minimum(lre_ref[...], -1e-4)
    li = lim_ref[...]
    dt = jnp.exp(ldt_ref[...])
    ar, ai = lr * dt, li * dt
    mag = jnp.exp(ar)
    lbr, lbi = mag * jnp.cos(ai), mag * jnp.sin(ai)
    lbr_ref[...] = lbr
    lbi_ref[...] = lbi
    mag_s = jnp.exp(sub_len * ar)
    lsr_ref[...] = mag_s * jnp.cos(sub_len * ai)
    lsi_ref[...] = mag_s * jnp.sin(sub_len * ai)
    nr, ni = lbr - 1.0, lbi
    den = lr * lr + li * li
    cr = (nr * lr + ni * li) / den
    ci = (ni * lr - nr * li) / den
    bre, bim = bre_ref[...], bim_ref[...]
    bbr_ref[...] = cr * bre - ci * bim
    bbi_ref[...] = cr * bim + ci * bre


def ssm_prepare(lam_re, lam_im, log_dt, b_re, b_im, sub_len):
    g, p = lam_re.shape
    h = b_re.shape[-1]
    gp = jax.ShapeDtypeStruct((g, 1, p), F32)
    ghp = jax.ShapeDtypeStruct((g, h, p), F32)
    return pl.pallas_call(
        functools.partial(_ssm_prep_kernel, sub_len=float(sub_len)),
        out_shape=[gp, gp, gp, gp, ghp, ghp],
        name="ssm_prepare",
    )(lam_re.reshape(g, 1, p), lam_im.reshape(g, 1, p), log_dt.reshape(g, 1, 1),
      b_re.transpose(0, 2, 1), b_im.transpose(0, 2, 1))


def _block_diag(w):
    g, a, b = w.shape
    n = GROUPS_PER_LANE_BLOCK
    w4 = w.reshape(g // n, n, a, b)
    eye = jnp.eye(n, dtype=w.dtype)
    return (w4[:, :, :, None, :] * eye[None, :, None, :, None]).reshape(g // n, n * a, n * b)


def _gelu_tanh(x):
    return 0.5 * x * (1.0 + jnp.tanh(math.sqrt(2.0 / math.pi) * (x + 0.044715 * (x * x * x))))


def _ssm_kernel(u_ref, bmat_ref, cmat_ref, lam_ref, lams_ref, d_ref, y_ref,
                perm_ref, x_ref, carry_ref, *, chunk, lane_blocks):
    sub = V7X_SUBLANES
    sub_len = chunk // sub
    n = STATE_COLS

    @pl.when(pl.program_id(1) == 0)
    def _():
        carry_ref[...] = jnp.zeros_like(carry_ref)

    for lb in range(lane_blocks):
        lanes = slice(lb * V7X_LANES, (lb + 1) * V7X_LANES)
        perm_ref[...] = pltpu.einshape(
            "rjl->jrl", u_ref[:, lanes].reshape(sub, sub_len, V7X_LANES)).reshape(chunk, V7X_LANES)
        x_ref[...] = _dot(perm_ref[...].astype(BF16), bmat_ref[lb])

        lam_r = jnp.broadcast_to(lam_ref[lb, :, 0:n], (sub, n))
        lam_i = jnp.broadcast_to(lam_ref[lb, :, n:2 * n], (sub, n))

        def step(j, xr, xi):
            rows = pl.ds(pl.multiple_of(j * sub, sub), sub)
            bur, bui = x_ref[rows, 0:n], x_ref[rows, n:2 * n]
            return rows, lam_r * xr - lam_i * xi + bur, lam_r * xi + lam_i * xr + bui

        def local_end(j, x):
            _, xr, xi = step(j, *x)
            return xr, xi
        zero = jnp.zeros((sub, n), F32)
        end_r, end_i = lax.fori_loop(0, sub_len, local_end, (zero, zero), unroll=2)

        ls_r, ls_i = lams_ref[lb, :, 0:n], lams_ref[lb, :, n:2 * n]
        cr, ci = carry_ref[lb, :, 0:n], carry_ref[lb, :, n:2 * n]
        init_r, init_i = [], []
        for r in range(sub):
            init_r.append(cr)
            init_i.append(ci)
            cr, ci = (ls_r * cr - ls_i * ci + end_r[r:r + 1], ls_r * ci + ls_i * cr + end_i[r:r + 1])
        carry_ref[lb, :, 0:n] = cr
        carry_ref[lb, :, n:2 * n] = ci

        def scan(j, x):
            rows, xr, xi = step(j, *x)
            x_ref[rows, 0:n] = xr
            x_ref[rows, n:2 * n] = xi
            return xr, xi
        lax.fori_loop(0, sub_len, scan,
                      (jnp.concatenate(init_r, axis=0), jnp.concatenate(init_i, axis=0)), unroll=2)

        y = _dot(x_ref[...].astype(BF16), cmat_ref[lb]) + d_ref[:, lanes] * perm_ref[...]
        y_ref[:, lanes] = pltpu.einshape(
            "jrl->rjl", _gelu_tanh(y).reshape(sub_len, sub, V7X_LANES)).reshape(chunk, V7X_LANES)


def s5_scan(u, prep, c_re, c_im, d_skip, chunk):
    l, w = u.shape
    lbr, lbi, lsr, lsi, bbr, bbi = prep
    g = lbr.shape[0]
    nb = g // GROUPS_PER_LANE_BLOCK
    n = STATE_COLS
    lane_blocks = _pow2_tile(S5_LANE_BLOCKS, nb)
    wide = lane_blocks * V7X_LANES
    bmat = jnp.concatenate([_block_diag(bbr), _block_diag(bbi)], axis=-1).astype(BF16)
    cmat = jnp.concatenate([_block_diag(c_re.transpose(0, 2, 1)),
                            _block_diag(-c_im.transpose(0, 2, 1))], axis=1).astype(BF16)
    lam = jnp.concatenate([lbr.reshape(nb, 1, n), lbi.reshape(nb, 1, n)], axis=-1)
    lams = jnp.concatenate([lsr.reshape(nb, 1, n), lsi.reshape(nb, 1, n)], axis=-1)
    per_block = lambda *tail: pl.BlockSpec((lane_blocks,) + tail, lambda b, c: (b, 0, 0))
    return pl.pallas_call(
        functools.partial(_ssm_kernel, chunk=chunk, lane_blocks=lane_blocks),
        grid=(nb // lane_blocks, l // chunk),
        in_specs=[pl.BlockSpec((chunk, wide), lambda b, c: (c, b)),
                  per_block(V7X_LANES, 2 * n), per_block(2 * n, V7X_LANES),
                  per_block(1, 2 * n), per_block(1, 2 * n),
                  pl.BlockSpec((1, wide), lambda b, c: (0, b))],
        out_specs=pl.BlockSpec((chunk, wide), lambda b, c: (c, b)),
        out_shape=jax.ShapeDtypeStruct((l, w), F32),
        scratch_shapes=[pltpu.VMEM((chunk, V7X_LANES), F32),
                        pltpu.VMEM((chunk, 2 * n), F32),
                        pltpu.VMEM((lane_blocks, 1, 2 * n), F32)],
        compiler_params=_params("parallel", "arbitrary"),
        name="s5_scan",
    )(u, bmat, cmat, lam, lams, d_skip.reshape(1, w))


def _mem_kv_kernel(mem_ref, g_ref, w_ref, o_ref):
    o_ref[...] = _dot(_rms(mem_ref[...], g_ref[...]).astype(BF16), w_ref[...]).astype(o_ref.dtype)


def memory_kv(mem, gain, wkv, layer):
    nm, d = mem.shape
    n = wkv.shape[2]
    bn = _pow2_tile(STREAM_COLS, n)
    return pl.pallas_call(
        _mem_kv_kernel,
        grid=(n // bn,),
        in_specs=[pl.BlockSpec((nm, d), lambda j: (0, 0)),
                  pl.BlockSpec((1, d), lambda j: (0, 0)),
                  pl.BlockSpec((None, d, bn), lambda j: (layer, 0, j))],
        out_specs=pl.BlockSpec((nm, bn), lambda j: (0, j)),
        out_shape=jax.ShapeDtypeStruct((nm, n), BF16),
        compiler_params=_params("arbitrary"),
        name="memory_kv",
    )(mem, gain.reshape(1, d), wkv)


def _xattn_kernel(xn_ref, h_ref, wq_ref, kv_ref, wo_ref, gpost_ref, gpre_ref, h_out, a_out):
    xw = XA_HEADS * XA_HEAD_DIM
    q = _dot(xn_ref[...], wq_ref[...]).astype(BF16)
    scale = XA_HEAD_DIM ** -0.5
    outs = []
    for hd in range(XA_HEADS):
        cols = slice(hd * XA_HEAD_DIM, (hd + 1) * XA_HEAD_DIM)
        s = _dot_nt(q[:, cols], kv_ref[:, cols]) * scale
        p = jnp.exp(s - jnp.max(s, axis=-1, keepdims=True))
        p = p / jnp.sum(p, axis=-1, keepdims=True)
        outs.append(_dot(p.astype(BF16), kv_ref[:, xw + hd * XA_HEAD_DIM:xw + (hd + 1) * XA_HEAD_DIM]))
    o = jnp.concatenate(outs, axis=1).astype(BF16)
    y = _dot(o, wo_ref[...])
    hn = h_ref[...] + _rms(y, gpost_ref[...])
    h_out[...] = hn
    a_out[...] = _rms(hn, gpre_ref[...]).astype(a_out.dtype)


def cross_attention_block(xn, h, wq, kv, wo, layer, g_post, g_pre):
    m, d = h.shape
    nm, kvw = kv.shape
    xw = wq.shape[2]
    bm = _pow2_tile(ELEMENTWISE_ROWS, m)
    row = pl.BlockSpec((bm, d), lambda i: (i, 0))
    gain = pl.BlockSpec((1, d), lambda i: (0, 0))
    return pl.pallas_call(
        _xattn_kernel,
        grid=(m // bm,),
        in_specs=[row, row,
                  pl.BlockSpec((None, d, xw), lambda i: (layer, 0, 0)),
                  pl.BlockSpec((nm, kvw), lambda i: (0, 0)),
                  pl.BlockSpec((None, xw, d), lambda i: (layer, 0, 0)),
                  gain, gain],
        out_specs=[row, row],
        out_shape=[jax.ShapeDtypeStruct((m, d), F32), jax.ShapeDtypeStruct((m, d), BF16)],
        compiler_params=_params("parallel"),
        name="cross_attention",
    )(xn, h, wq, kv, wo, g_post.reshape(1, d), g_pre.reshape(1, d))


def kernel(x, mem, positions, norm_gains, mem_norm_gain, ffn1_w_gu, ffn1_w_down, w_in, attn_sinks, w_attn_out, ssm_lambda_re, ssm_lambda_im, ssm_log_dt, ssm_b_re, ssm_b_im, ssm_c_re, ssm_c_im, ssm_d, w_glu, w_ssm_out, w_o, xa_wq, xa_wkv, xa_wo, ffn2_w_gu, ffn2_w_down):
    batch, l, d = x.shape
    assert batch == 1, "the kernels below walk one sequence"
    depth = norm_gains.shape[0]
    aw = w_attn_out.shape[1]
    sw = w_glu.shape[1]
    n_q = aw // HEAD_DIM
    kw = (w_in.shape[2] - aw - sw - 2 * d) // 2
    n_kv = kw // HEAD_DIM
    c_ssm = aw + 2 * kw
    c_gate = c_ssm + sw
    ssm_chunk = _pow2_tile(S5_CHUNK, l)
    sub_len = ssm_chunk // V7X_SUBLANES

    bf = lambda w: w.astype(BF16)
    ffn1_w_gu, ffn1_w_down, ffn2_w_gu, ffn2_w_down = bf(ffn1_w_gu), bf(ffn1_w_down), bf(ffn2_w_gu), bf(ffn2_w_down)
    w_in, w_attn_out, w_glu, w_ssm_out, w_o = bf(w_in), bf(w_attn_out), bf(w_glu), bf(w_ssm_out), bf(w_o)
    xa_wq, xa_wkv, xa_wo = bf(xa_wq), bf(xa_wkv), bf(xa_wo)

    h = x.reshape(l, d)
    mem2 = mem.reshape(mem.shape[1], d)
    tabs = rope_tables(positions.reshape(l))
    a = rmsnorm_bf16(h, norm_gains[0, 0])
    for i in range(depth):
        g = norm_gains[i]
        t = swiglu_up(a, ffn1_w_gu, i)
        h, u = matmul_residual_norm(t, ffn1_w_down, i, h, g[1], g[2], 0.5)
        qkv = matmul(u, w_in, i, 0, c_ssm, F32, "proj_qkv")
        s_in = matmul(u, w_in, i, c_ssm, sw, F32, "proj_ssm")
        gates = matmul(u, w_in, i, c_gate, 2 * d, F32, "proj_gates")
        attn_o = sliding_window_attention(qkv, attn_sinks[i], tabs, n_q, n_kv)
        prep = ssm_prepare(ssm_lambda_re[i], ssm_lambda_im[i], ssm_log_dt[i], ssm_b_re[i], ssm_b_im[i], sub_len)
        y_ssm = s5_scan(s_in, prep, ssm_c_re[i], ssm_c_im[i], ssm_d[i], ssm_chunk)
        glu_o = glu(y_ssm, w_glu, i)
        mixed = gated_mix(attn_o, glu_o, w_attn_out, w_ssm_out, i, gates)
        h, xn = matmul_residual_norm(mixed, w_o, i, h, g[3], g[4], 1.0)
        kv = memory_kv(mem2, mem_norm_gain[i], xa_wkv, i)
        h, a = cross_attention_block(xn, h, xa_wq, kv, xa_wo, i, g[5], g[6])
        t = swiglu_up(a, ffn2_w_gu, i)
        h, a = matmul_residual_norm(t, ffn2_w_down, i, h, g[7], norm_gains[i + 1, 0] if i + 1 < depth else None, 0.5)
    return h.reshape(batch, l, d)
```

```python
import functools
import math

import jax
import jax.numpy as jnp
from jax import lax
from jax.experimental import pallas as pl
from jax.experimental.pallas import tpu as pltpu

F32 = jnp.float32
BF16 = jnp.bfloat16

CHUNK = 64
WINDOW = 128
ATTN_BLOCK = 128
HEAD_DIM = 128
ROT_DIM = HEAD_DIM // 4
ROPE_THETA = 500000.0
SSM_GROUP_CH = 16
SSM_STATE = 64
XA_HEADS = 4
XA_HEAD_DIM = 128
RMS_EPS = 1e-6
MASK_VALUE = -1e30

V7X_LANES = 128
V7X_SUBLANES = 8
V7X_VMEM_LIMIT_BYTES = 52 * 1024 * 1024

GROUPS_PER_LANE_BLOCK = V7X_LANES // SSM_GROUP_CH
STATE_COLS = GROUPS_PER_LANE_BLOCK * SSM_STATE

STREAM_ROWS = 1024
STREAM_COLS = 512
WIDE_STREAM_COLS = 1024
RESIDENT_ROWS = 128
ELEMENTWISE_ROWS = 256
S5_CHUNK = 1024
S5_LANE_BLOCKS = 4


def _pow2_tile(limit, *sizes):
    t = 1
    while t * 2 <= limit and all(s % (t * 2) == 0 for s in sizes):
        t *= 2
    return t


def _params(*sem):
    return pltpu.CompilerParams(dimension_semantics=sem, vmem_limit_bytes=V7X_VMEM_LIMIT_BYTES)


def _rms(x, g):
    ms = jnp.mean(x * x, axis=-1, keepdims=True)
    return x * lax.rsqrt(ms + RMS_EPS) * g


def _dot(a, b):
    return jnp.dot(a, b, preferred_element_type=F32)


def _dot_nt(a, b):
    return lax.dot_general(a, b, (((1,), (1,)), ((), ())), preferred_element_type=F32)


def _layer_cols(layer, k, bn, off=0):
    return pl.BlockSpec((None, k, bn), lambda i, j: (layer, 0, j + off))


def _norm_kernel(x_ref, g_ref, o_ref):
    o_ref[...] = _rms(x_ref[...], g_ref[...]).astype(o_ref.dtype)


def rmsnorm_bf16(x, g):
    m, d = x.shape
    bm = _pow2_tile(ELEMENTWISE_ROWS, m)
    return pl.pallas_call(
        _norm_kernel,
        grid=(m // bm,),
        in_specs=[pl.BlockSpec((bm, d), lambda i: (i, 0)), pl.BlockSpec((1, d), lambda i: (0, 0))],
        out_specs=pl.BlockSpec((bm, d), lambda i: (i, 0)),
        out_shape=jax.ShapeDtypeStruct((m, d), BF16),
        compiler_params=_params("parallel"),
        name="rmsnorm",
    )(x, g.reshape(1, d))


def _mm_kernel(a_ref, w_ref, o_ref):
    o_ref[...] = _dot(a_ref[...], w_ref[0]).astype(o_ref.dtype)


def matmul(a, w, layer, col0, ncols, out_dtype, name):
    m, k = a.shape
    bm = _pow2_tile(STREAM_ROWS, m)
    bn = _pow2_tile(WIDE_STREAM_COLS, ncols)
    assert col0 % V7X_LANES == 0
    w_spec = pl.BlockSpec((pl.Element(1), pl.Element(k), pl.Element(bn)),
                          lambda i, j: (layer, 0, pl.multiple_of(col0 + j * bn, V7X_LANES)))
    return pl.pallas_call(
        _mm_kernel,
        grid=(m // bm, ncols // bn),
        in_specs=[pl.BlockSpec((bm, k), lambda i, j: (i, 0)), w_spec],
        out_specs=pl.BlockSpec((bm, bn), lambda i, j: (i, j)),
        out_shape=jax.ShapeDtypeStruct((m, ncols), out_dtype),
        compiler_params=_params("parallel", "arbitrary"),
        name=name,
    )(a, w)


def _swiglu_kernel(a_ref, wg_ref, wu_ref, o_ref):
    a = a_ref[...]
    gate = _dot(a, wg_ref[...])
    up = _dot(a, wu_ref[...])
    o_ref[...] = (gate * jax.nn.sigmoid(gate) * up).astype(o_ref.dtype)


def swiglu_up(a, w_gu, layer):
    m, k = a.shape
    f = w_gu.shape[2] // 2
    bm = _pow2_tile(STREAM_ROWS, m)
    bn = _pow2_tile(STREAM_COLS, f)
    return pl.pallas_call(
        _swiglu_kernel,
        grid=(m // bm, f // bn),
        in_specs=[pl.BlockSpec((bm, k), lambda i, j: (i, 0)),
                  _layer_cols(layer, k, bn), _layer_cols(layer, k, bn, f // bn)],
        out_specs=pl.BlockSpec((bm, bn), lambda i, j: (i, j)),
        out_shape=jax.ShapeDtypeStruct((m, f), BF16),
        compiler_params=_params("parallel", "arbitrary"),
        name="swiglu_up",
    )(a, w_gu, w_gu)


def _mm_res_norm_kernel(a_ref, w_ref, h_ref, gpost_ref, *rest, scale, with_norm):
    if with_norm:
        gpre_ref, h_out, a_out, y_even, y_odd = rest
    else:
        h_out, y_even, y_odd = rest
    s = pl.program_id(0)

    @pl.when(s == 0)
    def _():
        y_odd[...] = jnp.zeros_like(y_odd)

    def phase(y_new, y_old):
        y_new[...] = _dot(a_ref[...], w_ref[...])
        hn = h_ref[...] + scale * _rms(y_old[...], gpost_ref[...])
        h_out[...] = hn
        if with_norm:
            a_out[...] = _rms(hn, gpre_ref[...]).astype(a_out.dtype)

    @pl.when(s % 2 == 0)
    def _():
        phase(y_even, y_odd)

    @pl.when(s % 2 == 1)
    def _():
        phase(y_odd, y_even)


def matmul_residual_norm(a, w, layer, h, g_post, g_pre, scale):
    m, k = a.shape
    d = h.shape[1]
    bm = _pow2_tile(RESIDENT_ROWS, m)
    n_blocks = m // bm
    done = lambda s: (jnp.maximum(s - 1, 0), 0)
    row = pl.BlockSpec((bm, d), done)
    gain = pl.BlockSpec((1, d), lambda s: (0, 0))
    a_spec = pl.BlockSpec((bm, k), lambda s: (jnp.minimum(s, n_blocks - 1), 0))
    w_spec = pl.BlockSpec((None, k, d), lambda s: (layer, 0, 0), pipeline_mode=pl.Buffered(1))
    with_norm = g_pre is not None
    gains = [g_post.reshape(1, d)] + ([g_pre.reshape(1, d)] if with_norm else [])
    h_shape = jax.ShapeDtypeStruct((m, d), F32)
    out = pl.pallas_call(
        functools.partial(_mm_res_norm_kernel, scale=scale, with_norm=with_norm),
        grid=(n_blocks + 1,),
        in_specs=[a_spec, w_spec, row] + [gain] * len(gains),
        out_specs=[row, row] if with_norm else row,
        out_shape=[h_shape, jax.ShapeDtypeStruct((m, d), BF16)] if with_norm else h_shape,
        scratch_shapes=[pltpu.VMEM((bm, d), F32), pltpu.VMEM((bm, d), F32)],
        compiler_params=_params("arbitrary"),
        name="matmul_residual_norm",
    )(a, w, h, *gains)
    return out if with_norm else (out, None)


def _glu_kernel(y_ref, yj_ref, w_ref, o_ref):
    z = _dot(y_ref[...].astype(BF16), w_ref[...])
    o_ref[...] = (yj_ref[...] * jax.nn.sigmoid(z)).astype(o_ref.dtype)


def glu(y, w, layer):
    m, k = y.shape
    n = w.shape[2]
    bm = _pow2_tile(STREAM_ROWS, m)
    bn = _pow2_tile(WIDE_STREAM_COLS, n)
    return pl.pallas_call(
        _glu_kernel,
        grid=(m // bm, n // bn),
        in_specs=[pl.BlockSpec((bm, k), lambda i, j: (i, 0)),
                  pl.BlockSpec((bm, bn), lambda i, j: (i, j)),
                  _layer_cols(layer, k, bn)],
        out_specs=pl.BlockSpec((bm, bn), lambda i, j: (i, j)),
        out_shape=jax.ShapeDtypeStruct((m, n), BF16),
        compiler_params=_params("parallel", "arbitrary"),
        name="ssm_glu",
    )(y, y, w)


def _mix_kernel(ao_ref, gl_ref, wa_ref, ws_ref, ga_ref, gs_ref, o_ref):
    y_attn = _dot(ao_ref[...], wa_ref[...])
    y_ssm = _dot(gl_ref[...], ws_ref[...])
    o_ref[...] = (jax.nn.sigmoid(ga_ref[...]) * y_attn
                  + jax.nn.sigmoid(gs_ref[...]) * y_ssm).astype(o_ref.dtype)


def gated_mix(attn_o, glu_o, w_attn_out, w_ssm_out, layer, gates):
    m, ka = attn_o.shape
    ks = glu_o.shape[1]
    d = w_attn_out.shape[2]
    bm = _pow2_tile(STREAM_ROWS, m)
    bn = _pow2_tile(STREAM_COLS, d)
    return pl.pallas_call(
        _mix_kernel,
        grid=(m // bm, d // bn),
        in_specs=[pl.BlockSpec((bm, ka), lambda i, j: (i, 0)),
                  pl.BlockSpec((bm, ks), lambda i, j: (i, 0)),
                  _layer_cols(layer, ka, bn), _layer_cols(layer, ks, bn),
                  pl.BlockSpec((bm, bn), lambda i, j: (i, j)),
                  pl.BlockSpec((bm, bn), lambda i, j: (i, j + d // bn))],
        out_specs=pl.BlockSpec((bm, bn), lambda i, j: (i, j)),
        out_shape=jax.ShapeDtypeStruct((m, d), BF16),
        compiler_params=_params("parallel", "arbitrary"),
        name="gated_mix",
    )(attn_o, glu_o, w_attn_out, w_ssm_out, gates, gates)


def _rope_table_kernel(pos_ref, invf_ref, a_ref, b_ref, c_ref):
    ang = pos_ref[...].astype(F32) * invf_ref[...]
    cos, sin = jnp.cos(ang), jnp.sin(ang)
    lane = lax.broadcasted_iota(jnp.int32, ang.shape, 1)
    half = ROT_DIM // 2
    a_ref[...] = jnp.where(lane < ROT_DIM, cos, 1.0)
    b_ref[...] = jnp.where(lane < half, -sin, 0.0)
    c_ref[...] = jnp.where((lane >= half) & (lane < ROT_DIM), sin, 0.0)


def rope_tables(positions):
    l = positions.shape[0]
    inv_freq = ROPE_THETA ** (-jnp.arange(0, ROT_DIM, 2, dtype=F32) / ROT_DIM)
    invf = jnp.concatenate([inv_freq, inv_freq, jnp.zeros((HEAD_DIM - ROT_DIM,), F32)]).reshape(1, HEAD_DIM)
    bm = _pow2_tile(STREAM_ROWS, l)
    tab = pl.BlockSpec((bm, HEAD_DIM), lambda i: (i, 0))
    shape = jax.ShapeDtypeStruct((l, HEAD_DIM), F32)
    return pl.pallas_call(
        _rope_table_kernel,
        grid=(l // bm,),
        in_specs=[pl.BlockSpec((bm, 1), lambda i: (i, 0)), pl.BlockSpec((1, HEAD_DIM), lambda i: (0, 0))],
        out_specs=[tab, tab, tab],
        out_shape=[shape, shape, shape],
        compiler_params=_params("parallel"),
        name="rope_tables",
    )(positions.reshape(l, 1), invf)


def _rope(x, a, b, c):
    half = ROT_DIM // 2
    return x * a + pltpu.roll(x, HEAD_DIM - half, 1) * b + pltpu.roll(x, half, 1) * c


def _swa_kernel(sink_ref, q_ref, kp_ref, kc_ref, vp_ref, vc_ref, ap_ref, bp_ref, cp_ref,
                ac_ref, bc_ref, cc_ref, o_ref, *, n_kv, group):
    i = pl.program_id(0)
    blk = ATTN_BLOCK
    rows = group * blk
    ac, bc, cc = ac_ref[...], bc_ref[...], cc_ref[...]
    ap, bp, cp = ap_ref[...], bp_ref[...], cp_ref[...]

    q_idx = lax.broadcasted_iota(jnp.int32, (rows, 2 * blk), 0) & (blk - 1)
    k_idx = lax.broadcasted_iota(jnp.int32, (rows, 2 * blk), 1)
    chunk_shift = CHUNK.bit_length() - 1
    q_chunk = q_idx >> chunk_shift
    k_chunk = k_idx >> chunk_shift
    cpb = blk // CHUNK
    win = WINDOW // CHUNK
    band = (k_chunk >= q_chunk + cpb - win) & (k_chunk <= q_chunk + cpb)
    first_valid_key = jnp.where(i > 0, 0, blk)
    mask = band & (k_idx >= first_valid_key)
    scale = HEAD_DIM ** -0.5

    for kh in range(n_kv):
        ks = slice(kh * HEAD_DIM, (kh + 1) * HEAD_DIM)
        k = jnp.concatenate([_rope(kp_ref[:, ks], ap, bp, cp), _rope(kc_ref[:, ks], ac, bc, cc)], axis=0)
        v = jnp.concatenate([vp_ref[:, ks], vc_ref[:, ks]], axis=0).astype(BF16)
        heads = [kh * group + g for g in range(group)]
        q = jnp.concatenate([_rope(q_ref[:, h * HEAD_DIM:(h + 1) * HEAD_DIM], ac, bc, cc) for h in heads], axis=0)
        sink = jnp.concatenate([jnp.full((blk, 1), sink_ref[0, h], F32) for h in heads], axis=0)
        s = _dot_nt(q.astype(BF16), k.astype(BF16)) * scale
        s = jnp.where(mask, s, MASK_VALUE)
        m = jnp.maximum(jnp.max(s, axis=-1, keepdims=True), sink)
        p = jnp.exp(s - m)
        den = jnp.sum(p, axis=-1, keepdims=True) + jnp.exp(sink - m)
        o = _dot((p / den).astype(BF16), v)
        for g, h in enumerate(heads):
            o_ref[:, h * HEAD_DIM:(h + 1) * HEAD_DIM] = o[g * blk:(g + 1) * blk].astype(o_ref.dtype)


def sliding_window_attention(qkv, sinks, tabs, n_q, n_kv):
    l = qkv.shape[0]
    aw, kw = n_q * HEAD_DIM, n_kv * HEAD_DIM
    blk = ATTN_BLOCK
    k_col, v_col = aw // kw, aw // kw + 1
    assert aw % kw == 0
    prev = lambda i: jnp.maximum(i - 1, 0)
    tab_c = pl.BlockSpec((blk, HEAD_DIM), lambda i: (i, 0))
    tab_p = pl.BlockSpec((blk, HEAD_DIM), lambda i: (prev(i), 0))
    ta, tb, tc = tabs
    return pl.pallas_call(
        functools.partial(_swa_kernel, n_kv=n_kv, group=n_q // n_kv),
        grid=(l // blk,),
        in_specs=[pl.BlockSpec(memory_space=pltpu.SMEM),
                  pl.BlockSpec((blk, aw), lambda i: (i, 0)),
                  pl.BlockSpec((blk, kw), lambda i: (prev(i), k_col)),
                  pl.BlockSpec((blk, kw), lambda i: (i, k_col)),
                  pl.BlockSpec((blk, kw), lambda i: (prev(i), v_col)),
                  pl.BlockSpec((blk, kw), lambda i: (i, v_col)),
                  tab_p, tab_p, tab_p, tab_c, tab_c, tab_c],
        out_specs=pl.BlockSpec((blk, aw), lambda i: (i, 0)),
        out_shape=jax.ShapeDtypeStruct((l, aw), BF16),
        compiler_params=_params("parallel"),
        name="swa_attention",
    )(sinks.reshape(1, n_q), qkv, qkv, qkv, qkv, qkv, ta, tb, tc, ta, tb, tc)


def _ssm_prep_kernel(lre_ref, lim_ref, ldt_ref, bre_ref, bim_ref,
                     lbr_ref, lbi_ref, lsr_ref, lsi_ref, bbr_ref, bbi_ref, *, sub_len):
    lr = jnp.minimum(lre_ref[...], -1e-4)
    li = lim_ref[...]
    dt = jnp.exp(ldt_ref[...])
    ar, ai = lr * dt, li * dt
    mag = jnp.exp(ar)
    lbr, lbi = mag * jnp.cos(ai), mag * jnp.sin(ai)
    lbr_ref[...] = lbr
    lbi_ref[...] = lbi
    mag_s = jnp.exp(sub_len * ar)
    lsr_ref[...] = mag_s * jnp.cos(sub_len * ai)
    lsi_ref[...] = mag_s * jnp.sin(sub_len * ai)
    nr, ni = lbr - 1.0, lbi
    den = lr * lr + li * li
    cr = (nr * lr + ni * li) / den
    ci = (ni * lr - nr * li) / den
    bre, bim = bre_ref[...], bim_ref[...]
    bbr_ref[...] = cr * bre - ci * bim
    bbi_ref[...] = cr * bim + ci * bre


def ssm_prepare(lam_re, lam_im, log_dt, b_re, b_im, sub_len):
    g, p = lam_re.shape
    h = b_re.shape[-1]
    gp = jax.ShapeDtypeStruct((g, 1, p), F32)
    ghp = jax.ShapeDtypeStruct((g, h, p), F32)
    return pl.pallas_call(
        functools.partial(_ssm_prep_kernel, sub_len=float(sub_len)),
        out_shape=[gp, gp, gp, gp, ghp, ghp],
        name="ssm_prepare",
    )(lam_re.reshape(g, 1, p), lam_im.reshape(g, 1, p), log_dt.reshape(g, 1, 1),
      b_re.transpose(0, 2, 1), b_im.transpose(0, 2, 1))


def _block_diag(w):
    g, a, b = w.shape
    n = GROUPS_PER_LANE_BLOCK
    w4 = w.reshape(g // n, n, a, b)
    eye = jnp.eye(n, dtype=w.dtype)
    return (w4[:, :, :, None, :] * eye[None, :, None, :, None]).reshape(g // n, n * a, n * b)


def _gelu_tanh(x):
    return 0.5 * x * (1.0 + jnp.tanh(math.sqrt(2.0 / math.pi) * (x + 0.044715 * (x * x * x))))


def _ssm_kernel(u_ref, bmat_ref, cmat_ref, lam_ref, lams_ref, d_ref, y_ref,
                perm_ref, x_ref, carry_ref, *, chunk, lane_blocks):
    sub = V7X_SUBLANES
    sub_len = chunk // sub
    n = STATE_COLS

    @pl.when(pl.program_id(1) == 0)
    def _():
        carry_ref[...] = jnp.zeros_like(carry_ref)

    for lb in range(lane_blocks):
        lanes = slice(lb * V7X_LANES, (lb + 1) * V7X_LANES)
        perm_ref[...] = pltpu.einshape(
            "rjl->jrl", u_ref[:, lanes].reshape(sub, sub_len, V7X_LANES)).reshape(chunk, V7X_LANES)
        x_ref[...] = _dot(perm_ref[...].astype(BF16), bmat_ref[lb])

        lam_r = jnp.broadcast_to(lam_ref[lb, :, 0:n], (sub, n))
        lam_i = jnp.broadcast_to(lam_ref[lb, :, n:2 * n], (sub, n))

        def step(j, xr, xi):
            rows = pl.ds(pl.multiple_of(j * sub, sub), sub)
            bur, bui = x_ref[rows, 0:n], x_ref[rows, n:2 * n]
            return rows, lam_r * xr - lam_i * xi + bur, lam_r * xi + lam_i * xr + bui

        def local_end(j, x):
            _, xr, xi = step(j, *x)
            return xr, xi
        zero = jnp.zeros((sub, n), F32)
        end_r, end_i = lax.fori_loop(0, sub_len, local_end, (zero, zero), unroll=2)

        ls_r, ls_i = lams_ref[lb, :, 0:n], lams_ref[lb, :, n:2 * n]
        cr, ci = carry_ref[lb, :, 0:n], carry_ref[lb, :, n:2 * n]
        init_r, init_i = [], []
        for r in range(sub):
            init_r.append(cr)
            init_i.append(ci)
            cr, ci = (ls_r * cr - ls_i * ci + end_r[r:r + 1], ls_r * ci + ls_i * cr + end_i[r:r + 1])
        carry_ref[lb, :, 0:n] = cr
        carry_ref[lb, :, n:2 * n] = ci

        def scan(j, x):
            rows, xr, xi = step(j, *x)
            x_ref[rows, 0:n] = xr
            x_ref[rows, n:2 * n] = xi
            return xr, xi
        lax.fori_loop(0, sub_len, scan,
                      (jnp.concatenate(init_r, axis=0), jnp.concatenate(init_i, axis=0)), unroll=2)

        y = _dot(x_ref[...].astype(BF16), cmat_ref[lb]) + d_ref[:, lanes] * perm_ref[...]
        y_ref[:, lanes] = pltpu.einshape(
            "jrl->rjl", _gelu_tanh(y).reshape(sub_len, sub, V7X_LANES)).reshape(chunk, V7X_LANES)


def s5_scan(u, prep, c_re, c_im, d_skip, chunk):
    l, w = u.shape
    lbr, lbi, lsr, lsi, bbr, bbi = prep
    g = lbr.shape[0]
    nb = g // GROUPS_PER_LANE_BLOCK
    n = STATE_COLS
    lane_blocks = _pow2_tile(S5_LANE_BLOCKS, nb)
    wide = lane_blocks * V7X_LANES
    bmat = jnp.concatenate([_block_diag(bbr), _block_diag(bbi)], axis=-1).astype(BF16)
    cmat = jnp.concatenate([_block_diag(c_re.transpose(0, 2, 1)),
                            _block_diag(-c_im.transpose(0, 2, 1))], axis=1).astype(BF16)
    lam = jnp.concatenate([lbr.reshape(nb, 1, n), lbi.reshape(nb, 1, n)], axis=-1)
    lams = jnp.concatenate([lsr.reshape(nb, 1, n), lsi.reshape(nb, 1, n)], axis=-1)
    per_block = lambda *tail: pl.BlockSpec((lane_blocks,) + tail, lambda b, c: (b, 0, 0))
    return pl.pallas_call(
        functools.partial(_ssm_kernel, chunk=chunk, lane_blocks=lane_blocks),
        grid=(nb // lane_blocks, l // chunk),
        in_specs=[pl.BlockSpec((chunk, wide), lambda b, c: (c, b)),
                  per_block(V7X_LANES, 2 * n), per_block(2 * n, V7X_LANES),
                  per_block(1, 2 * n), per_block(1, 2 * n),
                  pl.BlockSpec((1, wide), lambda b, c: (0, b))],
        out_specs=pl.BlockSpec((chunk, wide), lambda b, c: (c, b)),
        out_shape=jax.ShapeDtypeStruct((l, w), F32),
        scratch_shapes=[pltpu.VMEM((chunk, V7X_LANES), F32),
                        pltpu.VMEM((chunk, 2 * n), F32),
                        pltpu.VMEM((lane_blocks, 1, 2 * n), F32)],
        compiler_params=_params("parallel", "arbitrary"),
        name="s5_scan",
    )(u, bmat, cmat, lam, lams, d_skip.reshape(1, w))


def _mem_kv_kernel(mem_ref, g_ref, w_ref, o_ref):
    o_ref[...] = _dot(_rms(mem_ref[...], g_ref[...]).astype(BF16), w_ref[...]).astype(o_ref.dtype)


def memory_kv(mem, gain, wkv, layer):
    nm, d = mem.shape
    n = wkv.shape[2]
    bn = _pow2_tile(STREAM_COLS, n)
    return pl.pallas_call(
        _mem_kv_kernel,
        grid=(n // bn,),
        in_specs=[pl.BlockSpec((nm, d), lambda j: (0, 0)),
                  pl.BlockSpec((1, d), lambda j: (0, 0)),
                  pl.BlockSpec((None, d, bn), lambda j: (layer, 0, j))],
        out_specs=pl.BlockSpec((nm, bn), lambda j: (0, j)),
        out_shape=jax.ShapeDtypeStruct((nm, n), BF16),
        compiler_params=_params("arbitrary"),
        name="memory_kv",
    )(mem, gain.reshape(1, d), wkv)


def _xattn_kernel(xn_ref, h_ref, wq_ref, kv_ref, wo_ref, gpost_ref, gpre_ref, h_out, a_out):
    xw = XA_HEADS * XA_HEAD_DIM
    q = _dot(xn_ref[...], wq_ref[...]).astype(BF16)
    scale = XA_HEAD_DIM ** -0.5
    outs = []
    for hd in range(XA_HEADS):
        cols = slice(hd * XA_HEAD_DIM, (hd + 1) * XA_HEAD_DIM)
        s = _dot_nt(q[:, cols], kv_ref[:, cols]) * scale
        p = jnp.exp(s - jnp.max(s, axis=-1, keepdims=True))
        p = p / jnp.sum(p, axis=-1, keepdims=True)
        outs.append(_dot(p.astype(BF16), kv_ref[:, xw + hd * XA_HEAD_DIM:xw + (hd + 1) * XA_HEAD_DIM]))
    o = jnp.concatenate(outs, axis=1).astype(BF16)
    y = _dot(o, wo_ref[...])
    hn = h_ref[...] + _rms(y, gpost_ref[...])
    h_out[...] = hn
    a_out[...] = _rms(hn, gpre_ref[...]).astype(a_out.dtype)


def cross_attention_block(xn, h, wq, kv, wo, layer, g_post, g_pre):
    m, d = h.shape
    nm, kvw = kv.shape
    xw = wq.shape[2]
    bm = _pow2_tile(ELEMENTWISE_ROWS, m)
    row = pl.BlockSpec((bm, d), lambda i: (i, 0))
    gain = pl.BlockSpec((1, d), lambda i: (0, 0))
    return pl.pallas_call(
        _xattn_kernel,
        grid=(m // bm,),
        in_specs=[row, row,
                  pl.BlockSpec((None, d, xw), lambda i: (layer, 0, 0)),
                  pl.BlockSpec((nm, kvw), lambda i: (0, 0)),
                  pl.BlockSpec((None, xw, d), lambda i: (layer, 0, 0)),
                  gain, gain],
        out_specs=[row, row],
        out_shape=[jax.ShapeDtypeStruct((m, d), F32), jax.ShapeDtypeStruct((m, d), BF16)],
        compiler_params=_params("parallel"),
        name="cross_attention",
    )(xn, h, wq, kv, wo, g_post.reshape(1, d), g_pre.reshape(1, d))


def kernel(x, mem, positions, norm_gains, mem_norm_gain, ffn1_w_gu, ffn1_w_down, w_in, attn_sinks, w_attn_out, ssm_lambda_re, ssm_lambda_im, ssm_log_dt, ssm_b_re, ssm_b_im, ssm_c_re, ssm_c_im, ssm_d, w_glu, w_ssm_out, w_o, xa_wq, xa_wkv, xa_wo, ffn2_w_gu, ffn2_w_down):
    batch, l, d = x.shape
    assert batch == 1, "the kernels below walk one sequence"
    depth = norm_gains.shape[0]
    aw = w_attn_out.shape[1]
    sw = w_glu.shape[1]
    n_q = aw // HEAD_DIM
    kw = (w_in.shape[2] - aw - sw - 2 * d) // 2
    n_kv = kw // HEAD_DIM
    c_ssm = aw + 2 * kw
    c_gate = c_ssm + sw
    ssm_chunk = _pow2_tile(S5_CHUNK, l)
    sub_len = ssm_chunk // V7X_SUBLANES

    bf = lambda w: w.astype(BF16)
    ffn1_w_gu, ffn1_w_down, ffn2_w_gu, ffn2_w_down = bf(ffn1_w_gu), bf(ffn1_w_down), bf(ffn2_w_gu), bf(ffn2_w_down)
    w_in, w_attn_out, w_glu, w_ssm_out, w_o = bf(w_in), bf(w_attn_out), bf(w_glu), bf(w_ssm_out), bf(w_o)
    xa_wq, xa_wkv, xa_wo = bf(xa_wq), bf(xa_wkv), bf(xa_wo)

    h = x.reshape(l, d)
    mem2 = mem.reshape(mem.shape[1], d)
    tabs = rope_tables(positions.reshape(l))
    a = rmsnorm_bf16(h, norm_gains[0, 0])
    for i in range(depth):
        g = norm_gains[i]
        t = swiglu_up(a, ffn1_w_gu, i)
        h, u = matmul_residual_norm(t, ffn1_w_down, i, h, g[1], g[2], 0.5)
        qkv = matmul(u, w_in, i, 0, c_ssm, F32, "proj_qkv")
        s_in = matmul(u, w_in, i, c_ssm, sw, F32, "proj_ssm")
        gates = matmul(u, w_in, i, c_gate, 2 * d, F32, "proj_gates")
        attn_o = sliding_window_attention(qkv, attn_sinks[i], tabs, n_q, n_kv)
        prep = ssm_prepare(ssm_lambda_re[i], ssm_lambda_im[i], ssm_log_dt[i], ssm_b_re[i], ssm_b_im[i], sub_len)
        y_ssm = s5_scan(s_in, prep, ssm_c_re[i], ssm_c_im[i], ssm_d[i], ssm_chunk)
        glu_o = glu(y_ssm, w_glu, i)
        mixed = gated_mix(attn_o, glu_o, w_attn_out, w_ssm_out, i, gates)
        h, xn = matmul_residual_norm(mixed, w_o, i, h, g[3], g[4], 1.0)
        kv = memory_kv(mem2, mem_norm_gain[i], xa_wkv, i)
        h, a = cross_attention_block(xn, h, xa_wq, kv, xa_wo, i, g[5], g[6])
        t = swiglu_up(a, ffn2_w_gu, i)
        h, a = matmul_residual_norm(t, ffn2_w_down, i, h, g[7], norm_gains[i + 1, 0] if i + 1 < depth else None, 0.5)
    return h.reshape(batch, l, d)
```

```python
import functools
import math

import jax
import jax.numpy as jnp
from jax import lax
from jax.experimental import pallas as pl
from jax.experimental.pallas import tpu as pltpu

F32 = jnp.float32
BF16 = jnp.bfloat16

CHUNK = 64
WINDOW = 128
ATTN_BLOCK = 128
HEAD_DIM = 128
ROT_DIM = HEAD_DIM // 4
ROPE_THETA = 500000.0
SSM_GROUP_CH = 16
SSM_STATE = 64
XA_HEADS = 4
XA_HEAD_DIM = 128
RMS_EPS = 1e-6
MASK_VALUE = -1e30

V7X_LANES = 128
V7X_SUBLANES = 8
V7X_VMEM_LIMIT_BYTES = 52 * 1024 * 1024

GROUPS_PER_LANE_BLOCK = V7X_LANES // SSM_GROUP_CH
STATE_COLS = GROUPS_PER_LANE_BLOCK * SSM_STATE

STREAM_ROWS = 1024
STREAM_COLS = 512
WIDE_STREAM_COLS = 1024
RESIDENT_ROWS = 128
ELEMENTWISE_ROWS = 256
S5_CHUNK = 1024
S5_LANE_BLOCKS = 4


def _pow2_tile(limit, *sizes):
    t = 1
    while t * 2 <= limit and all(s % (t * 2) == 0 for s in sizes):
        t *= 2
    return t


def _params(*sem):
    return pltpu.CompilerParams(dimension_semantics=sem, vmem_limit_bytes=V7X_VMEM_LIMIT_BYTES)


def _rms(x, g):
    ms = jnp.mean(x * x, axis=-1, keepdims=True)
    return x * lax.rsqrt(ms + RMS_EPS) * g


def _dot(a, b):
    return jnp.dot(a, b, preferred_element_type=F32)


def _dot_nt(a, b):
    return lax.dot_general(a, b, (((1,), (1,)), ((), ())), preferred_element_type=F32)


def _layer_cols(layer, k, bn, off=0):
    return pl.BlockSpec((None, k, bn), lambda i, j: (layer, 0, j + off))


def _norm_kernel(x_ref, g_ref, o_ref):
    o_ref[...] = _rms(x_ref[...], g_ref[...]).astype(o_ref.dtype)


def rmsnorm_bf16(x, g):
    m, d = x.shape
    bm = _pow2_tile(ELEMENTWISE_ROWS, m)
    return pl.pallas_call(
        _norm_kernel,
        grid=(m // bm,),
        in_specs=[pl.BlockSpec((bm, d), lambda i: (i, 0)), pl.BlockSpec((1, d), lambda i: (0, 0))],
        out_specs=pl.BlockSpec((bm, d), lambda i: (i, 0)),
        out_shape=jax.ShapeDtypeStruct((m, d), BF16),
        compiler_params=_params("parallel"),
        name="rmsnorm",
    )(x, g.reshape(1, d))


def _mm_kernel(a_ref, w_ref, o_ref):
    o_ref[...] = _dot(a_ref[...], w_ref[0]).astype(o_ref.dtype)


def matmul(a, w, layer, col0, ncols, out_dtype, name):
    m, k = a.shape
    bm = _pow2_tile(STREAM_ROWS, m)
    bn = _pow2_tile(WIDE_STREAM_COLS, ncols)
    assert col0 % V7X_LANES == 0
    w_spec = pl.BlockSpec((pl.Element(1), pl.Element(k), pl.Element(bn)),
                          lambda i, j: (layer, 0, pl.multiple_of(col0 + j * bn, V7X_LANES)))
    return pl.pallas_call(
        _mm_kernel,
        grid=(m // bm, ncols // bn),
        in_specs=[pl.BlockSpec((bm, k), lambda i, j: (i, 0)), w_spec],
        out_specs=pl.BlockSpec((bm, bn), lambda i, j: (i, j)),
        out_shape=jax.ShapeDtypeStruct((m, ncols), out_dtype),
        compiler_params=_params("parallel", "arbitrary"),
        name=name,
    )(a, w)


def _swiglu_kernel(a_ref, wg_ref, wu_ref, o_ref):
    a = a_ref[...]
    gate = _dot(a, wg_ref[...])
    up = _dot(a, wu_ref[...])
    o_ref[...] = (gate * jax.nn.sigmoid(gate) * up).astype(o_ref.dtype)


def swiglu_up(a, w_gu, layer):
    m, k = a.shape
    f = w_gu.shape[2] // 2
    bm = _pow2_tile(STREAM_ROWS, m)
    bn = _pow2_tile(STREAM_COLS, f)
    return pl.pallas_call(
        _swiglu_kernel,
        grid=(m // bm, f // bn),
        in_specs=[pl.BlockSpec((bm, k), lambda i, j: (i, 0)),
                  _layer_cols(layer, k, bn), _layer_cols(layer, k, bn, f // bn)],
        out_specs=pl.BlockSpec((bm, bn), lambda i, j: (i, j)),
        out_shape=jax.ShapeDtypeStruct((m, f), BF16),
        compiler_params=_params("parallel", "arbitrary"),
        name="swiglu_up",
    )(a, w_gu, w_gu)


def _mm_res_norm_kernel(a_ref, w_ref, h_ref, gpost_ref, *rest, scale, with_norm):
    if with_norm:
        gpre_ref, h_out, a_out, y_even, y_odd = rest
    else:
        h_out, y_even, y_odd = rest
    s = pl.program_id(0)

    @pl.when(s == 0)
    def _():
        y_odd[...] = jnp.zeros_like(y_odd)

    def phase(y_new, y_old):
        y_new[...] = _dot(a_ref[...], w_ref[...])
        hn = h_ref[...] + scale * _rms(y_old[...], gpost_ref[...])
        h_out[...] = hn
        if with_norm:
            a_out[...] = _rms(hn, gpre_ref[...]).astype(a_out.dtype)

    @pl.when(s % 2 == 0)
    def _():
        phase(y_even, y_odd)

    @pl.when(s % 2 == 1)
    def _():
        phase(y_odd, y_even)


def matmul_residual_norm(a, w, layer, h, g_post, g_pre, scale):
    m, k = a.shape
    d = h.shape[1]
    bm = _pow2_tile(RESIDENT_ROWS, m)
    n_blocks = m // bm
    done = lambda s: (jnp.maximum(s - 1, 0), 0)
    row = pl.BlockSpec((bm, d), done)
    gain = pl.BlockSpec((1, d), lambda s: (0, 0))
    a_spec = pl.BlockSpec((bm, k), lambda s: (jnp.minimum(s, n_blocks - 1), 0))
    w_spec = pl.BlockSpec((None, k, d), lambda s: (layer, 0, 0), pipeline_mode=pl.Buffered(1))
    with_norm = g_pre is not None
    gains = [g_post.reshape(1, d)] + ([g_pre.reshape(1, d)] if with_norm else [])
    h_shape = jax.ShapeDtypeStruct((m, d), F32)
    out = pl.pallas_call(
        functools.partial(_mm_res_norm_kernel, scale=scale, with_norm=with_norm),
        grid=(n_blocks + 1,),
        in_specs=[a_spec, w_spec, row] + [gain] * len(gains),
        out_specs=[row, row] if with_norm else row,
        out_shape=[h_shape, jax.ShapeDtypeStruct((m, d), BF16)] if with_norm else h_shape,
        scratch_shapes=[pltpu.VMEM((bm, d), F32), pltpu.VMEM((bm, d), F32)],
        compiler_params=_params("arbitrary"),
        name="matmul_residual_norm",
    )(a, w, h, *gains)
    return out if with_norm else (out, None)


def _glu_kernel(y_ref, yj_ref, w_ref, o_ref):
    z = _dot(y_ref[...].astype(BF16), w_ref[...])
    o_ref[...] = (yj_ref[...] * jax.nn.sigmoid(z)).astype(o_ref.dtype)


def glu(y, w, layer):
    m, k = y.shape
    n = w.shape[2]
    bm = _pow2_tile(STREAM_ROWS, m)
    bn = _pow2_tile(WIDE_STREAM_COLS, n)
    return pl.pallas_call(
        _glu_kernel,
        grid=(m // bm, n // bn),
        in_specs=[pl.BlockSpec((bm, k), lambda i, j: (i, 0)),
                  pl.BlockSpec((bm, bn), lambda i, j: (i, j)),
                  _layer_cols(layer, k, bn)],
        out_specs=pl.BlockSpec((bm, bn), lambda i, j: (i, j)),
        out_shape=jax.ShapeDtypeStruct((m, n), BF16),
        compiler_params=_params("parallel", "arbitrary"),
        name="ssm_glu",
    )(y, y, w)


def _mix_kernel(ao_ref, gl_ref, wa_ref, ws_ref, ga_ref, gs_ref, o_ref):
    y_attn = _dot(ao_ref[...], wa_ref[...])
    y_ssm = _dot(gl_ref[...], ws_ref[...])
    o_ref[...] = (jax.nn.sigmoid(ga_ref[...]) * y_attn
                  + jax.nn.sigmoid(gs_ref[...]) * y_ssm).astype(o_ref.dtype)


def gated_mix(attn_o, glu_o, w_attn_out, w_ssm_out, layer, gates):
    m, ka = attn_o.shape
    ks = glu_o.shape[1]
    d = w_attn_out.shape[2]
    bm = _pow2_tile(STREAM_ROWS, m)
    bn = _pow2_tile(STREAM_COLS, d)
    return pl.pallas_call(
        _mix_kernel,
        grid=(m // bm, d // bn),
        in_specs=[pl.BlockSpec((bm, ka), lambda i, j: (i, 0)),
                  pl.BlockSpec((bm, ks), lambda i, j: (i, 0)),
                  _layer_cols(layer, ka, bn), _layer_cols(layer, ks, bn),
                  pl.BlockSpec((bm, bn), lambda i, j: (i, j)),
                  pl.BlockSpec((bm, bn), lambda i, j: (i, j + d // bn))],
        out_specs=pl.BlockSpec((bm, bn), lambda i, j: (i, j)),
        out_shape=jax.ShapeDtypeStruct((m, d), BF16),
        compiler_params=_params("parallel", "arbitrary"),
        name="gated_mix",
    )(attn_o, glu_o, w_attn_out, w_ssm_out, gates, gates)


def _rope_table_kernel(pos_ref, invf_ref, a_ref, b_ref, c_ref):
    ang = pos_ref[...].astype(F32) * invf_ref[...]
    cos, sin = jnp.cos(ang), jnp.sin(ang)
    lane = lax.broadcasted_iota(jnp.int32, ang.shape, 1)
    half = ROT_DIM // 2
    a_ref[...] = jnp.where(lane < ROT_DIM, cos, 1.0)
    b_ref[...] = jnp.where(lane < half, -sin, 0.0)
    c_ref[...] = jnp.where((lane >= half) & (lane < ROT_DIM), sin, 0.0)


def rope_tables(positions):
    l = positions.shape[0]
    inv_freq = ROPE_THETA ** (-jnp.arange(0, ROT_DIM, 2, dtype=F32) / ROT_DIM)
    invf = jnp.concatenate([inv_freq, inv_freq, jnp.zeros((HEAD_DIM - ROT_DIM,), F32)]).reshape(1, HEAD_DIM)
    bm = _pow2_tile(STREAM_ROWS, l)
    tab = pl.BlockSpec((bm, HEAD_DIM), lambda i: (i, 0))
    shape = jax.ShapeDtypeStruct((l, HEAD_DIM), F32)
    return pl.pallas_call(
        _rope_table_kernel,
        grid=(l // bm,),
        in_specs=[pl.BlockSpec((bm, 1), lambda i: (i, 0)), pl.BlockSpec((1, HEAD_DIM), lambda i: (0, 0))],
        out_specs=[tab, tab, tab],
        out_shape=[shape, shape, shape],
        compiler_params=_params("parallel"),
        name="rope_tables",
    )(positions.reshape(l, 1), invf)


def _rope(x, a, b, c):
    half = ROT_DIM // 2
    return x * a + pltpu.roll(x, HEAD_DIM - half, 1) * b + pltpu.roll(x, half, 1) * c


def _swa_kernel(sink_ref, q_ref, kp_ref, kc_ref, vp_ref, vc_ref, ap_ref, bp_ref, cp_ref,
                ac_ref, bc_ref, cc_ref, o_ref, *, n_kv, group):
    i = pl.program_id(0)
    blk = ATTN_BLOCK
    rows = group * blk
    ac, bc, cc = ac_ref[...], bc_ref[...], cc_ref[...]
    ap, bp, cp = ap_ref[...], bp_ref[...], cp_ref[...]

    q_idx = lax.broadcasted_iota(jnp.int32, (rows, 2 * blk), 0) & (blk - 1)
    k_idx = lax.broadcasted_iota(jnp.int32, (rows, 2 * blk), 1)
    chunk_shift = CHUNK.bit_length() - 1
    q_chunk = q_idx >> chunk_shift
    k_chunk = k_idx >> chunk_shift
    cpb = blk // CHUNK
    win = WINDOW // CHUNK
    band = (k_chunk >= q_chunk + cpb - win) & (k_chunk <= q_chunk + cpb)
    first_valid_key = jnp.where(i > 0, 0, blk)
    mask = band & (k_idx >= first_valid_key)
    scale = HEAD_DIM ** -0.5

    for kh in range(n_kv):
        ks = slice(kh * HEAD_DIM, (kh + 1) * HEAD_DIM)
        k = jnp.concatenate([_rope(kp_ref[:, ks], ap, bp, cp), _rope(kc_ref[:, ks], ac, bc, cc)], axis=0)
        v = jnp.concatenate([vp_ref[:, ks], vc_ref[:, ks]], axis=0).astype(BF16)
        heads = [kh * group + g for g in range(group)]
        q = jnp.concatenate([_rope(q_ref[:, h * HEAD_DIM:(h + 1) * HEAD_DIM], ac, bc, cc) for h in heads], axis=0)
        sink = jnp.concatenate([jnp.full((blk, 1), sink_ref[0, h], F32) for h in heads], axis=0)
        s = _dot_nt(q.astype(BF16), k.astype(BF16)) * scale
        s = jnp.where(mask, s, MASK_VALUE)
        m = jnp.maximum(jnp.max(s, axis=-1, keepdims=True), sink)
        p = jnp.exp(s - m)
        den = jnp.sum(p, axis=-1, keepdims=True) + jnp.exp(sink - m)
        o = _dot((p / den).astype(BF16), v)
        for g, h in enumerate(heads):
            o_ref[:, h * HEAD_DIM:(h + 1) * HEAD_DIM] = o[g * blk:(g + 1) * blk].astype(o_ref.dtype)


def sliding_window_attention(qkv, sinks, tabs, n_q, n_kv):
    l = qkv.shape[0]
    aw, kw = n_q * HEAD_DIM, n_kv * HEAD_DIM
    blk = ATTN_BLOCK
    k_col, v_col = aw // kw, aw // kw + 1
    assert aw % kw == 0
    prev = lambda i: jnp.maximum(i - 1, 0)
    tab_c = pl.BlockSpec((blk, HEAD_DIM), lambda i: (i, 0))
    tab_p = pl.BlockSpec((blk, HEAD_DIM), lambda i: (prev(i), 0))
    ta, tb, tc = tabs
    return pl.pallas_call(
        functools.partial(_swa_kernel, n_kv=n_kv, group=n_q // n_kv),
        grid=(l // blk,),
        in_specs=[pl.BlockSpec(memory_space=pltpu.SMEM),
                  pl.BlockSpec((blk, aw), lambda i: (i, 0)),
                  pl.BlockSpec((blk, kw), lambda i: (prev(i), k_col)),
                  pl.BlockSpec((blk, kw), lambda i: (i, k_col)),
                  pl.BlockSpec((blk, kw), lambda i: (prev(i), v_col)),
                  pl.BlockSpec((blk, kw), lambda i: (i, v_col)),
                  tab_p, tab_p, tab_p, tab_c, tab_c, tab_c],
        out_specs=pl.BlockSpec((blk, aw), lambda i: (i, 0)),
        out_shape=jax.ShapeDtypeStruct((l, aw), BF16),
        compiler_params=_params("parallel"),
        name="swa_attention",
    )(sinks.reshape(1, n_q), qkv, qkv, qkv, qkv, qkv, ta, tb, tc, ta, tb, tc)


def _ssm_prep_kernel(lre_ref, lim_ref, ldt_ref, bre_ref, bim_ref,
                     lbr_ref, lbi_ref, lsr_ref, lsi_ref, bbr_ref, bbi_ref, *, sub_len):
    lr = jnp.minimum(lre_ref[...], -1e-4)
    li = lim_ref[...]
    dt = jnp.exp(ldt_ref[...])
    ar, ai = lr * dt, li * dt
    mag = jnp.exp(ar)
    lbr, lbi = mag * jnp.cos(ai), mag * jnp.sin(ai)
    lbr_ref[...] = lbr
    lbi_ref[...] = lbi
    mag_s = jnp.exp(sub_len * ar)
    lsr_ref[...] = mag_s * jnp.cos(sub_len * ai)
    lsi_ref[...] = mag_s * jnp.sin(sub_len * ai)
    nr, ni = lbr - 1.0, lbi
    den = lr * lr + li * li
    cr = (nr * lr + ni * li) / den
    ci = (ni * lr - nr * li) / den
    bre, bim = bre_ref[...], bim_ref[...]
    bbr_ref[...] = cr * bre - ci * bim
    bbi_ref[...] = cr * bim + ci * bre


def ssm_prepare(lam_re, lam_im, log_dt, b_re, b_im, sub_len):
    g, p = lam_re.shape
    h = b_re.shape[-1]
    gp = jax.ShapeDtypeStruct((g, 1, p), F32)
    ghp = jax.ShapeDtypeStruct((g, h, p), F32)
    return pl.pallas_call(
        functools.partial(_ssm_prep_kernel, sub_len=float(sub_len)),
        out_shape=[gp, gp, gp, gp, ghp, ghp],
        name="ssm_prepare",
    )(lam_re.reshape(g, 1, p), lam_im.reshape(g, 1, p), log_dt.reshape(g, 1, 1),
      b_re.transpose(0, 2, 1), b_im.transpose(0, 2, 1))


def _block_diag(w):
    g, a, b = w.shape
    n = GROUPS_PER_LANE_BLOCK
    w4 = w.reshape(g // n, n, a, b)
    eye = jnp.eye(n, dtype=w.dtype)
    return (w4[:, :, :, None, :] * eye[None, :, None, :, None]).reshape(g // n, n * a, n * b)


def _gelu_tanh(x):
    return 0.5 * x * (1.0 + jnp.tanh(math.sqrt(2.0 / math.pi) * (x + 0.044715 * (x * x * x))))


def _ssm_kernel(u_ref, bmat_ref, cmat_ref, lam_ref, lams_ref, d_ref, y_ref,
                perm_ref, x_ref, carry_ref, *, chunk, lane_blocks):
    sub = V7X_SUBLANES
    sub_len = chunk // sub
    n = STATE_COLS

    @pl.when(pl.program_id(1) == 0)
    def _():
        carry_ref[...] = jnp.zeros_like(carry_ref)

    for lb in range(lane_blocks):
        lanes = slice(lb * V7X_LANES, (lb + 1) * V7X_LANES)
        perm_ref[...] = pltpu.einshape(
            "rjl->jrl", u_ref[:, lanes].reshape(sub, sub_len, V7X_LANES)).reshape(chunk, V7X_LANES)
        x_ref[...] = _dot(perm_ref[...].astype(BF16), bmat_ref[lb])

        lam_r = jnp.broadcast_to(lam_ref[lb, :, 0:n], (sub, n))
        lam_i = jnp.broadcast_to(lam_ref[lb, :, n:2 * n], (sub, n))

        def step(j, xr, xi):
            rows = pl.ds(j * sub, sub)
            bur, bui = x_ref[rows, 0:n], x_ref[rows, n:2 * n]
            return rows, lam_r * xr - lam_i * xi + bur, lam_r * xi + lam_i * xr + bui

        end_r = end_i = jnp.zeros((sub, n), F32)
        for j in range(sub_len):
            _, end_r, end_i = step(j, end_r, end_i)

        ls_r, ls_i = lams_ref[lb, :, 0:n], lams_ref[lb, :, n:2 * n]
        cr, ci = carry_ref[lb, :, 0:n], carry_ref[lb, :, n:2 * n]
        init_r, init_i = [], []
        for r in range(sub):
            init_r.append(cr)
            init_i.append(ci)
            cr, ci = (ls_r * cr - ls_i * ci + end_r[r:r + 1], ls_r * ci + ls_i * cr + end_i[r:r + 1])
        carry_ref[lb, :, 0:n] = cr
        carry_ref[lb, :, n:2 * n] = ci

        xr, xi = jnp.concatenate(init_r, axis=0), jnp.concatenate(init_i, axis=0)
        for j in range(sub_len):
            rows, xr, xi = step(j, xr, xi)
            x_ref[rows, 0:n] = xr
            x_ref[rows, n:2 * n] = xi

        y = _dot(x_ref[...].astype(BF16), cmat_ref[lb]) + d_ref[:, lanes] * perm_ref[...]
        y_ref[:, lanes] = pltpu.einshape(
            "jrl->rjl", _gelu_tanh(y).reshape(sub_len, sub, V7X_LANES)).reshape(chunk, V7X_LANES)


def s5_scan(u, prep, c_re, c_im, d_skip, chunk):
    l, w = u.shape
    lbr, lbi, lsr, lsi, bbr, bbi = prep
    g = lbr.shape[0]
    nb = g // GROUPS_PER_LANE_BLOCK
    n = STATE_COLS
    lane_blocks = _pow2_tile(S5_LANE_BLOCKS, nb)
    wide = lane_blocks * V7X_LANES
    bmat = jnp.concatenate([_block_diag(bbr), _block_diag(bbi)], axis=-1).astype(BF16)
    cmat = jnp.concatenate([_block_diag(c_re.transpose(0, 2, 1)),
                            _block_diag(-c_im.transpose(0, 2, 1))], axis=1).astype(BF16)
    lam = jnp.concatenate([lbr.reshape(nb, 1, n), lbi.reshape(nb, 1, n)], axis=-1)
    lams = jnp.concatenate([lsr.reshape(nb, 1, n), lsi.reshape(nb, 1, n)], axis=-1)
    per_block = lambda *tail: pl.BlockSpec((lane_blocks,) + tail, lambda b, c: (b, 0, 0))
    return pl.pallas_call(
        functools.partial(_ssm_kernel, chunk=chunk, lane_blocks=lane_blocks),
        grid=(nb // lane_blocks, l // chunk),
        in_specs=[pl.BlockSpec((chunk, wide), lambda b, c: (c, b)),
                  per_block(V7X_LANES, 2 * n), per_block(2 * n, V7X_LANES),
                  per_block(1, 2 * n), per_block(1, 2 * n),
                  pl.BlockSpec((1, wide), lambda b, c: (0, b))],
        out_specs=pl.BlockSpec((chunk, wide), lambda b, c: (c, b)),
        out_shape=jax.ShapeDtypeStruct((l, w), F32),
        scratch_shapes=[pltpu.VMEM((chunk, V7X_LANES), F32),
                        pltpu.VMEM((chunk, 2 * n), F32),
                        pltpu.VMEM((lane_blocks, 1, 2 * n), F32)],
        compiler_params=_params("parallel", "arbitrary"),
        name="s5_scan",
    )(u, bmat, cmat, lam, lams, d_skip.reshape(1, w))


def _mem_kv_kernel(mem_ref, g_ref, w_ref, o_ref):
    o_ref[...] = _dot(_rms(mem_ref[...], g_ref[...]).astype(BF16), w_ref[...]).astype(o_ref.dtype)


def memory_kv(mem, gain, wkv, layer):
    nm, d = mem.shape
    n = wkv.shape[2]
    bn = _pow2_tile(STREAM_COLS, n)
    return pl.pallas_call(
        _mem_kv_kernel,
        grid=(n // bn,),
        in_specs=[pl.BlockSpec((nm, d), lambda j: (0, 0)),
                  pl.BlockSpec((1, d), lambda j: (0, 0)),
                  pl.BlockSpec((None, d, bn), lambda j: (layer, 0, j))],
        out_specs=pl.BlockSpec((nm, bn), lambda j: (0, j)),
        out_shape=jax.ShapeDtypeStruct((nm, n), BF16),
        compiler_params=_params("arbitrary"),
        name="memory_kv",
    )(mem, gain.reshape(1, d), wkv)


def _xattn_kernel(xn_ref, h_ref, wq_ref, kv_ref, wo_ref, gpost_ref, gpre_ref, h_out, a_out):
    xw = XA_HEADS * XA_HEAD_DIM
    q = _dot(xn_ref[...], wq_ref[...]).astype(BF16)
    scale = XA_HEAD_DIM ** -0.5
    outs = []
    for hd in range(XA_HEADS):
        cols = slice(hd * XA_HEAD_DIM, (hd + 1) * XA_HEAD_DIM)
        s = _dot_nt(q[:, cols], kv_ref[:, cols]) * scale
        p = jnp.exp(s - jnp.max(s, axis=-1, keepdims=True))
        p = p / jnp.sum(p, axis=-1, keepdims=True)
        outs.append(_dot(p.astype(BF16), kv_ref[:, xw + hd * XA_HEAD_DIM:xw + (hd + 1) * XA_HEAD_DIM]))
    o = jnp.concatenate(outs, axis=1).astype(BF16)
    y = _dot(o, wo_ref[...])
    hn = h_ref[...] + _rms(y, gpost_ref[...])
    h_out[...] = hn
    a_out[...] = _rms(hn, gpre_ref[...]).astype(a_out.dtype)


def cross_attention_block(xn, h, wq, kv, wo, layer, g_post, g_pre):
    m, d = h.shape
    nm, kvw = kv.shape
    xw = wq.shape[2]
    bm = _pow2_tile(ELEMENTWISE_ROWS, m)
    row = pl.BlockSpec((bm, d), lambda i: (i, 0))
    gain = pl.BlockSpec((1, d), lambda i: (0, 0))
    return pl.pallas_call(
        _xattn_kernel,
        grid=(m // bm,),
        in_specs=[row, row,
                  pl.BlockSpec((None, d, xw), lambda i: (layer, 0, 0)),
                  pl.BlockSpec((nm, kvw), lambda i: (0, 0)),
                  pl.BlockSpec((None, xw, d), lambda i: (layer, 0, 0)),
                  gain, gain],
        out_specs=[row, row],
        out_shape=[jax.ShapeDtypeStruct((m, d), F32), jax.ShapeDtypeStruct((m, d), BF16)],
        compiler_params=_params("parallel"),
        name="cross_attention",
    )(xn, h, wq, kv, wo, g_post.reshape(1, d), g_pre.reshape(1, d))


def kernel(x, mem, positions, norm_gains, mem_norm_gain, ffn1_w_gu, ffn1_w_down, w_in, attn_sinks, w_attn_out, ssm_lambda_re, ssm_lambda_im, ssm_log_dt, ssm_b_re, ssm_b_im, ssm_c_re, ssm_c_im, ssm_d, w_glu, w_ssm_out, w_o, xa_wq, xa_wkv, xa_wo, ffn2_w_gu, ffn2_w_down):
    batch, l, d = x.shape
    assert batch == 1, "the kernels below walk one sequence"
    depth = norm_gains.shape[0]
    aw = w_attn_out.shape[1]
    sw = w_glu.shape[1]
    n_q = aw // HEAD_DIM
    kw = (w_in.shape[2] - aw - sw - 2 * d) // 2
    n_kv = kw // HEAD_DIM
    c_ssm = aw + 2 * kw
    c_gate = c_ssm + sw
    ssm_chunk = _pow2_tile(S5_CHUNK, l)
    sub_len = ssm_chunk // V7X_SUBLANES

    bf = lambda w: w.astype(BF16)
    ffn1_w_gu, ffn1_w_down, ffn2_w_gu, ffn2_w_down = bf(ffn1_w_gu), bf(ffn1_w_down), bf(ffn2_w_gu), bf(ffn2_w_down)
    w_in, w_attn_out, w_glu, w_ssm_out, w_o = bf(w_in), bf(w_attn_out), bf(w_glu), bf(w_ssm_out), bf(w_o)
    xa_wq, xa_wkv, xa_wo = bf(xa_wq), bf(xa_wkv), bf(xa_wo)

    h = x.reshape(l, d)
    mem2 = mem.reshape(mem.shape[1], d)
    tabs = rope_tables(positions.reshape(l))
    a = rmsnorm_bf16(h, norm_gains[0, 0])
    for i in range(depth):
        g = norm_gains[i]
        t = swiglu_up(a, ffn1_w_gu, i)
        h, u = matmul_residual_norm(t, ffn1_w_down, i, h, g[1], g[2], 0.5)
        qkv = matmul(u, w_in, i, 0, c_ssm, F32, "proj_qkv")
        s_in = matmul(u, w_in, i, c_ssm, sw, F32, "proj_ssm")
        gates = matmul(u, w_in, i, c_gate, 2 * d, F32, "proj_gates")
        attn_o = sliding_window_attention(qkv, attn_sinks[i], tabs, n_q, n_kv)
        prep = ssm_prepare(ssm_lambda_re[i], ssm_lambda_im[i], ssm_log_dt[i], ssm_b_re[i], ssm_b_im[i], sub_len)
        y_ssm = s5_scan(s_in, prep, ssm_c_re[i], ssm_c_im[i], ssm_d[i], ssm_chunk)
        glu_o = glu(y_ssm, w_glu, i)
        mixed = gated_mix(attn_o, glu_o, w_attn_out, w_ssm_out, i, gates)
        h, xn = matmul_residual_norm(mixed, w_o, i, h, g[3], g[4], 1.0)
        kv = memory_kv(mem2, mem_norm_gain[i], xa_wkv, i)
        h, a = cross_attention_block(xn, h, xa_wq, kv, xa_wo, i, g[5], g[6])
        t = swiglu_up(a, ffn2_w_gu, i)
        h, a = matmul_residual_norm(t, ffn2_w_down, i, h, g[7], norm_gains[i + 1, 0] if i + 1 < depth else None, 0.5)
    return h.reshape(batch, l, d)
```

```python
import functools
import math

import jax
import jax.numpy as jnp
from jax import lax
from jax.experimental import pallas as pl
from jax.experimental.pallas import tpu as pltpu

F32 = jnp.float32
BF16 = jnp.bfloat16

CHUNK = 64
WINDOW = 128
ATTN_BLOCK = 128
HEAD_DIM = 128
ROT_DIM = HEAD_DIM // 4
ROPE_THETA = 500000.0
SSM_GROUP_CH = 16
SSM_STATE = 64
XA_HEADS = 4
XA_HEAD_DIM = 128
RMS_EPS = 1e-6
MASK_VALUE = -1e30

V7X_LANES = 128
V7X_SUBLANES = 8
V7X_VMEM_LIMIT_BYTES = 52 * 1024 * 1024

GROUPS_PER_LANE_BLOCK = V7X_LANES // SSM_GROUP_CH
STATE_COLS = GROUPS_PER_LANE_BLOCK * SSM_STATE

STREAM_ROWS = 1024
STREAM_COLS = 512
WIDE_STREAM_COLS = 1024
RESIDENT_ROWS = 128
ELEMENTWISE_ROWS = 256
S5_CHUNK = 1024
S5_LANE_BLOCKS = 8


def _pow2_tile(limit, *sizes):
    t = 1
    while t * 2 <= limit and all(s % (t * 2) == 0 for s in sizes):
        t *= 2
    return t


def _params(*sem):
    return pltpu.CompilerParams(dimension_semantics=sem, vmem_limit_bytes=V7X_VMEM_LIMIT_BYTES)


def _rms(x, g):
    ms = jnp.mean(x * x, axis=-1, keepdims=True)
    return x * lax.rsqrt(ms + RMS_EPS) * g


def _dot(a, b):
    return jnp.dot(a, b, preferred_element_type=F32)


def _dot_nt(a, b):
    return lax.dot_general(a, b, (((1,), (1,)), ((), ())), preferred_element_type=F32)


def _layer_cols(layer, k, bn, off=0):
    return pl.BlockSpec((None, k, bn), lambda i, j: (layer, 0, j + off))


def _norm_kernel(x_ref, g_ref, o_ref):
    o_ref[...] = _rms(x_ref[...], g_ref[...]).astype(o_ref.dtype)


def rmsnorm_bf16(x, g):
    m, d = x.shape
    bm = _pow2_tile(ELEMENTWISE_ROWS, m)
    return pl.pallas_call(
        _norm_kernel,
        grid=(m // bm,),
        in_specs=[pl.BlockSpec((bm, d), lambda i: (i, 0)), pl.BlockSpec((1, d), lambda i: (0, 0))],
        out_specs=pl.BlockSpec((bm, d), lambda i: (i, 0)),
        out_shape=jax.ShapeDtypeStruct((m, d), BF16),
        compiler_params=_params("parallel"),
        name="rmsnorm",
    )(x, g.reshape(1, d))


def _mm_kernel(a_ref, w_ref, o_ref):
    o_ref[...] = _dot(a_ref[...], w_ref[0]).astype(o_ref.dtype)


def matmul(a, w, layer, col0, ncols, out_dtype, name):
    m, k = a.shape
    bm = _pow2_tile(STREAM_ROWS, m)
    bn = _pow2_tile(WIDE_STREAM_COLS, ncols)
    assert col0 % V7X_LANES == 0
    w_spec = pl.BlockSpec((pl.Element(1), pl.Element(k), pl.Element(bn)),
                          lambda i, j: (layer, 0, pl.multiple_of(col0 + j * bn, V7X_LANES)))
    return pl.pallas_call(
        _mm_kernel,
        grid=(m // bm, ncols // bn),
        in_specs=[pl.BlockSpec((bm, k), lambda i, j: (i, 0)), w_spec],
        out_specs=pl.BlockSpec((bm, bn), lambda i, j: (i, j)),
        out_shape=jax.ShapeDtypeStruct((m, ncols), out_dtype),
        compiler_params=_params("parallel", "arbitrary"),
        name=name,
    )(a, w)


def _swiglu_kernel(a_ref, wg_ref, wu_ref, o_ref):
    a = a_ref[...]
    gate = _dot(a, wg_ref[...])
    up = _dot(a, wu_ref[...])
    o_ref[...] = (gate * jax.nn.sigmoid(gate) * up).astype(o_ref.dtype)


def swiglu_up(a, w_gu, layer):
    m, k = a.shape
    f = w_gu.shape[2] // 2
    bm = _pow2_tile(STREAM_ROWS, m)
    bn = _pow2_tile(STREAM_COLS, f)
    return pl.pallas_call(
        _swiglu_kernel,
        grid=(m // bm, f // bn),
        in_specs=[pl.BlockSpec((bm, k), lambda i, j: (i, 0)),
                  _layer_cols(layer, k, bn), _layer_cols(layer, k, bn, f // bn)],
        out_specs=pl.BlockSpec((bm, bn), lambda i, j: (i, j)),
        out_shape=jax.ShapeDtypeStruct((m, f), BF16),
        compiler_params=_params("parallel", "arbitrary"),
        name="swiglu_up",
    )(a, w_gu, w_gu)


def _mm_res_norm_kernel(a_ref, w_ref, h_ref, gpost_ref, *rest, scale, with_norm):
    if with_norm:
        gpre_ref, h_out, a_out, y_even, y_odd = rest
    else:
        h_out, y_even, y_odd = rest
    s = pl.program_id(0)

    @pl.when(s == 0)
    def _():
        y_odd[...] = jnp.zeros_like(y_odd)

    def phase(y_new, y_old):
        y_new[...] = _dot(a_ref[...], w_ref[...])
        hn = h_ref[...] + scale * _rms(y_old[...], gpost_ref[...])
        h_out[...] = hn
        if with_norm:
            a_out[...] = _rms(hn, gpre_ref[...]).astype(a_out.dtype)

    @pl.when(s % 2 == 0)
    def _():
        phase(y_even, y_odd)

    @pl.when(s % 2 == 1)
    def _():
        phase(y_odd, y_even)


def matmul_residual_norm(a, w, layer, h, g_post, g_pre, scale):
    m, k = a.shape
    d = h.shape[1]
    bm = _pow2_tile(RESIDENT_ROWS, m)
    n_blocks = m // bm
    done = lambda s: (jnp.maximum(s - 1, 0), 0)
    row = pl.BlockSpec((bm, d), done)
    gain = pl.BlockSpec((1, d), lambda s: (0, 0))
    a_spec = pl.BlockSpec((bm, k), lambda s: (jnp.minimum(s, n_blocks - 1), 0))
    w_spec = pl.BlockSpec((None, k, d), lambda s: (layer, 0, 0), pipeline_mode=pl.Buffered(1))
    with_norm = g_pre is not None
    gains = [g_post.reshape(1, d)] + ([g_pre.reshape(1, d)] if with_norm else [])
    h_shape = jax.ShapeDtypeStruct((m, d), F32)
    out = pl.pallas_call(
        functools.partial(_mm_res_norm_kernel, scale=scale, with_norm=with_norm),
        grid=(n_blocks + 1,),
        in_specs=[a_spec, w_spec, row] + [gain] * len(gains),
        out_specs=[row, row] if with_norm else row,
        out_shape=[h_shape, jax.ShapeDtypeStruct((m, d), BF16)] if with_norm else h_shape,
        scratch_shapes=[pltpu.VMEM((bm, d), F32), pltpu.VMEM((bm, d), F32)],
        compiler_params=_params("arbitrary"),
        name="matmul_residual_norm",
    )(a, w, h, *gains)
    return out if with_norm else (out, None)


def _glu_kernel(y_ref, yj_ref, w_ref, o_ref):
    z = _dot(y_ref[...].astype(BF16), w_ref[...])
    o_ref[...] = (yj_ref[...] * jax.nn.sigmoid(z)).astype(o_ref.dtype)


def glu(y, w, layer):
    m, k = y.shape
    n = w.shape[2]
    bm = _pow2_tile(STREAM_ROWS, m)
    bn = _pow2_tile(WIDE_STREAM_COLS, n)
    return pl.pallas_call(
        _glu_kernel,
        grid=(m // bm, n // bn),
        in_specs=[pl.BlockSpec((bm, k), lambda i, j: (i, 0)),
                  pl.BlockSpec((bm, bn), lambda i, j: (i, j)),
                  _layer_cols(layer, k, bn)],
        out_specs=pl.BlockSpec((bm, bn), lambda i, j: (i, j)),
        out_shape=jax.ShapeDtypeStruct((m, n), BF16),
        compiler_params=_params("parallel", "arbitrary"),
        name="ssm_glu",
    )(y, y, w)


def _mix_kernel(ao_ref, gl_ref, wa_ref, ws_ref, ga_ref, gs_ref, o_ref):
    y_attn = _dot(ao_ref[...], wa_ref[...])
    y_ssm = _dot(gl_ref[...], ws_ref[...])
    o_ref[...] = (jax.nn.sigmoid(ga_ref[...]) * y_attn
                  + jax.nn.sigmoid(gs_ref[...]) * y_ssm).astype(o_ref.dtype)


def gated_mix(attn_o, glu_o, w_attn_out, w_ssm_out, layer, gates):
    m, ka = attn_o.shape
    ks = glu_o.shape[1]
    d = w_attn_out.shape[2]
    bm = _pow2_tile(STREAM_ROWS, m)
    bn = _pow2_tile(STREAM_COLS, d)
    return pl.pallas_call(
        _mix_kernel,
        grid=(m // bm, d // bn),
        in_specs=[pl.BlockSpec((bm, ka), lambda i, j: (i, 0)),
                  pl.BlockSpec((bm, ks), lambda i, j: (i, 0)),
                  _layer_cols(layer, ka, bn), _layer_cols(layer, ks, bn),
                  pl.BlockSpec((bm, bn), lambda i, j: (i, j)),
                  pl.BlockSpec((bm, bn), lambda i, j: (i, j + d // bn))],
        out_specs=pl.BlockSpec((bm, bn), lambda i, j: (i, j)),
        out_shape=jax.ShapeDtypeStruct((m, d), BF16),
        compiler_params=_params("parallel", "arbitrary"),
        name="gated_mix",
    )(attn_o, glu_o, w_attn_out, w_ssm_out, gates, gates)


def _rope_table_kernel(pos_ref, invf_ref, a_ref, b_ref, c_ref):
    ang = pos_ref[...].astype(F32) * invf_ref[...]
    cos, sin = jnp.cos(ang), jnp.sin(ang)
    lane = lax.broadcasted_iota(jnp.int32, ang.shape, 1)
    half = ROT_DIM // 2
    a_ref[...] = jnp.where(lane < ROT_DIM, cos, 1.0)
    b_ref[...] = jnp.where(lane < half, -sin, 0.0)
    c_ref[...] = jnp.where((lane >= half) & (lane < ROT_DIM), sin, 0.0)


def rope_tables(positions):
    l = positions.shape[0]
    inv_freq = ROPE_THETA ** (-jnp.arange(0, ROT_DIM, 2, dtype=F32) / ROT_DIM)
    invf = jnp.concatenate([inv_freq, inv_freq, jnp.zeros((HEAD_DIM - ROT_DIM,), F32)]).reshape(1, HEAD_DIM)
    bm = _pow2_tile(STREAM_ROWS, l)
    tab = pl.BlockSpec((bm, HEAD_DIM), lambda i: (i, 0))
    shape = jax.ShapeDtypeStruct((l, HEAD_DIM), F32)
    return pl.pallas_call(
        _rope_table_kernel,
        grid=(l // bm,),
        in_specs=[pl.BlockSpec((bm, 1), lambda i: (i, 0)), pl.BlockSpec((1, HEAD_DIM), lambda i: (0, 0))],
        out_specs=[tab, tab, tab],
        out_shape=[shape, shape, shape],
        compiler_params=_params("parallel"),
        name="rope_tables",
    )(positions.reshape(l, 1), invf)


def _rope(x, a, b, c):
    half = ROT_DIM // 2
    return x * a + pltpu.roll(x, HEAD_DIM - half, 1) * b + pltpu.roll(x, half, 1) * c


def _swa_kernel(sink_ref, q_ref, kp_ref, kc_ref, vp_ref, vc_ref, ap_ref, bp_ref, cp_ref,
                ac_ref, bc_ref, cc_ref, o_ref, *, n_kv, group):
    i = pl.program_id(0)
    blk = ATTN_BLOCK
    rows = group * blk
    ac, bc, cc = ac_ref[...], bc_ref[...], cc_ref[...]
    ap, bp, cp = ap_ref[...], bp_ref[...], cp_ref[...]

    q_idx = lax.broadcasted_iota(jnp.int32, (rows, 2 * blk), 0) & (blk - 1)
    k_idx = lax.broadcasted_iota(jnp.int32, (rows, 2 * blk), 1)
    chunk_shift = CHUNK.bit_length() - 1
    q_chunk = q_idx >> chunk_shift
    k_chunk = k_idx >> chunk_shift
    cpb = blk // CHUNK
    win = WINDOW // CHUNK
    band = (k_chunk >= q_chunk + cpb - win) & (k_chunk <= q_chunk + cpb)
    first_valid_key = jnp.where(i > 0, 0, blk)
    mask = band & (k_idx >= first_valid_key)
    scale = HEAD_DIM ** -0.5

    for kh in range(n_kv):
        ks = slice(kh * HEAD_DIM, (kh + 1) * HEAD_DIM)
        k = jnp.concatenate([_rope(kp_ref[:, ks], ap, bp, cp), _rope(kc_ref[:, ks], ac, bc, cc)], axis=0)
        v = jnp.concatenate([vp_ref[:, ks], vc_ref[:, ks]], axis=0).astype(BF16)
        heads = [kh * group + g for g in range(group)]
        q = jnp.concatenate([_rope(q_ref[:, h * HEAD_DIM:(h + 1) * HEAD_DIM], ac, bc, cc) for h in heads], axis=0)
        sink = jnp.concatenate([jnp.full((blk, 1), sink_ref[0, h], F32) for h in heads], axis=0)
        s = _dot_nt(q.astype(BF16), k.astype(BF16)) * scale
        s = jnp.where(mask, s, MASK_VALUE)
        m = jnp.maximum(jnp.max(s, axis=-1, keepdims=True), sink)
        p = jnp.exp(s - m)
        den = jnp.sum(p, axis=-1, keepdims=True) + jnp.exp(sink - m)
        o = _dot((p / den).astype(BF16), v)
        for g, h in enumerate(heads):
            o_ref[:, h * HEAD_DIM:(h + 1) * HEAD_DIM] = o[g * blk:(g + 1) * blk].astype(o_ref.dtype)


def sliding_window_attention(qkv, sinks, tabs, n_q, n_kv):
    l = qkv.shape[0]
    aw, kw = n_q * HEAD_DIM, n_kv * HEAD_DIM
    blk = ATTN_BLOCK
    k_col, v_col = aw // kw, aw // kw + 1
    assert aw % kw == 0
    prev = lambda i: jnp.maximum(i - 1, 0)
    tab_c = pl.BlockSpec((blk, HEAD_DIM), lambda i: (i, 0))
    tab_p = pl.BlockSpec((blk, HEAD_DIM), lambda i: (prev(i), 0))
    ta, tb, tc = tabs
    return pl.pallas_call(
        functools.partial(_swa_kernel, n_kv=n_kv, group=n_q // n_kv),
        grid=(l // blk,),
        in_specs=[pl.BlockSpec(memory_space=pltpu.SMEM),
                  pl.BlockSpec((blk, aw), lambda i: (i, 0)),
                  pl.BlockSpec((blk, kw), lambda i: (prev(i), k_col)),
                  pl.BlockSpec((blk, kw), lambda i: (i, k_col)),
                  pl.BlockSpec((blk, kw), lambda i: (prev(i), v_col)),
                  pl.BlockSpec((blk, kw), lambda i: (i, v_col)),
                  tab_p, tab_p, tab_p, tab_c, tab_c, tab_c],
        out_specs=pl.BlockSpec((blk, aw), lambda i: (i, 0)),
        out_shape=jax.ShapeDtypeStruct((l, aw), BF16),
        compiler_params=_params("parallel"),
        name="swa_attention",
    )(sinks.reshape(1, n_q), qkv, qkv, qkv, qkv, qkv, ta, tb, tc, ta, tb, tc)


def _ssm_prep_kernel(lre_ref, lim_ref, ldt_ref, bre_ref, bim_ref,
                     lbr_ref, lbi_ref, lsr_ref, lsi_ref, bbr_ref, bbi_ref, *, sub_len):
    lr = jnp.minimum(lre_ref[...], -1e-4)
    li = lim_ref[...]
    dt = jnp.exp(ldt_ref[...])
    ar, ai = lr * dt, li * dt
    mag = jnp.exp(ar)
    lbr, lbi = mag * jnp.cos(ai), mag * jnp.sin(ai)
    lbr_ref[...] = lbr
    lbi_ref[...] = lbi
    mag_s = jnp.exp(sub_len * ar)
    lsr_ref[...] = mag_s * jnp.cos(sub_len * ai)
    lsi_ref[...] = mag_s * jnp.sin(sub_len * ai)
    nr, ni = lbr - 1.0, lbi
    den = lr * lr + li * li
    cr = (nr * lr + ni * li) / den
    ci = (ni * lr - nr * li) / den
    bre, bim = bre_ref[...], bim_ref[...]
    bbr_ref[...] = cr * bre - ci * bim
    bbi_ref[...] = cr * bim + ci * bre


def ssm_prepare(lam_re, lam_im, log_dt, b_re, b_im, sub_len):
    g, p = lam_re.shape
    h = b_re.shape[-1]
    gp = jax.ShapeDtypeStruct((g, 1, p), F32)
    ghp = jax.ShapeDtypeStruct((g, h, p), F32)
    return pl.pallas_call(
        functools.partial(_ssm_prep_kernel, sub_len=float(sub_len)),
        out_shape=[gp, gp, gp, gp, ghp, ghp],
        name="ssm_prepare",
    )(lam_re.reshape(g, 1, p), lam_im.reshape(g, 1, p), log_dt.reshape(g, 1, 1),
      b_re.transpose(0, 2, 1), b_im.transpose(0, 2, 1))


def _block_diag(w):
    g, a, b = w.shape
    n = GROUPS_PER_LANE_BLOCK
    w4 = w.reshape(g // n, n, a, b)
    eye = jnp.eye(n, dtype=w.dtype)
    return (w4[:, :, :, None, :] * eye[None, :, None, :, None]).reshape(g // n, n * a, n * b)


def _gelu_tanh(x):
    return 0.5 * x * (1.0 + jnp.tanh(math.sqrt(2.0 / math.pi) * (x + 0.044715 * (x * x * x))))


def _ssm_kernel(u_ref, bmat_ref, cmat_ref, lam_ref, lams_ref, d_ref, y_ref,
                perm_ref, x_ref, carry_ref, *, chunk, lane_blocks):
    sub = V7X_SUBLANES
    sub_len = chunk // sub
    n = STATE_COLS

    @pl.when(pl.program_id(1) == 0)
    def _():
        carry_ref[...] = jnp.zeros_like(carry_ref)

    for lb in range(lane_blocks):
        lanes = slice(lb * V7X_LANES, (lb + 1) * V7X_LANES)
        perm_ref[...] = pltpu.einshape(
            "rjl->jrl", u_ref[:, lanes].reshape(sub, sub_len, V7X_LANES)).reshape(chunk, V7X_LANES)
        x_ref[...] = _dot(perm_ref[...].astype(BF16), bmat_ref[lb])

        lam_r = jnp.broadcast_to(lam_ref[lb, :, 0:n], (sub, n))
        lam_i = jnp.broadcast_to(lam_ref[lb, :, n:2 * n], (sub, n))

        def step(j, xr, xi):
            rows = pl.ds(j * sub, sub)
            bur, bui = x_ref[rows, 0:n], x_ref[rows, n:2 * n]
            return rows, lam_r * xr - lam_i * xi + bur, lam_r * xi + lam_i * xr + bui

        end_r = end_i = jnp.zeros((sub, n), F32)
        for j in range(sub_len):
            _, end_r, end_i = step(j, end_r, end_i)

        ls_r, ls_i = lams_ref[lb, :, 0:n], lams_ref[lb, :, n:2 * n]
        cr, ci = carry_ref[lb, :, 0:n], carry_ref[lb, :, n:2 * n]
        init_r, init_i = [], []
        for r in range(sub):
            init_r.append(cr)
            init_i.append(ci)
            cr, ci = (ls_r * cr - ls_i * ci + end_r[r:r + 1], ls_r * ci + ls_i * cr + end_i[r:r + 1])
        carry_ref[lb, :, 0:n] = cr
        carry_ref[lb, :, n:2 * n] = ci

        xr, xi = jnp.concatenate(init_r, axis=0), jnp.concatenate(init_i, axis=0)
        for j in range(sub_len):
            rows, xr, xi = step(j, xr, xi)
            x_ref[rows, 0:n] = xr
            x_ref[rows, n:2 * n] = xi

        y = _dot(x_ref[...].astype(BF16), cmat_ref[lb]) + d_ref[:, lanes] * perm_ref[...]
        y_ref[:, lanes] = pltpu.einshape(
            "jrl->rjl", _gelu_tanh(y).reshape(sub_len, sub, V7X_LANES)).reshape(chunk, V7X_LANES)


def s5_scan(u, prep, c_re, c_im, d_skip, chunk):
    l, w = u.shape
    lbr, lbi, lsr, lsi, bbr, bbi = prep
    g = lbr.shape[0]
    nb = g // GROUPS_PER_LANE_BLOCK
    n = STATE_COLS
    lane_blocks = _pow2_tile(S5_LANE_BLOCKS, nb)
    wide = lane_blocks * V7X_LANES
    bmat = jnp.concatenate([_block_diag(bbr), _block_diag(bbi)], axis=-1).astype(BF16)
    cmat = jnp.concatenate([_block_diag(c_re.transpose(0, 2, 1)),
                            _block_diag(-c_im.transpose(0, 2, 1))], axis=1).astype(BF16)
    lam = jnp.concatenate([lbr.reshape(nb, 1, n), lbi.reshape(nb, 1, n)], axis=-1)
    lams = jnp.concatenate([lsr.reshape(nb, 1, n), lsi.reshape(nb, 1, n)], axis=-1)
    per_block = lambda *tail: pl.BlockSpec((lane_blocks,) + tail, lambda b, c: (b, 0, 0))
    return pl.pallas_call(
        functools.partial(_ssm_kernel, chunk=chunk, lane_blocks=lane_blocks),
        grid=(nb // lane_blocks, l // chunk),
        in_specs=[pl.BlockSpec((chunk, wide), lambda b, c: (c, b)),
                  per_block(V7X_LANES, 2 * n), per_block(2 * n, V7X_LANES),
                  per_block(1, 2 * n), per_block(1, 2 * n),
                  pl.BlockSpec((1, wide), lambda b, c: (0, b))],
        out_specs=pl.BlockSpec((chunk, wide), lambda b, c: (c, b)),
        out_shape=jax.ShapeDtypeStruct((l, w), F32),
        scratch_shapes=[pltpu.VMEM((chunk, V7X_LANES), F32),
                        pltpu.VMEM((chunk, 2 * n), F32),
                        pltpu.VMEM((lane_blocks, 1, 2 * n), F32)],
        compiler_params=_params("parallel", "arbitrary"),
        name="s5_scan",
    )(u, bmat, cmat, lam, lams, d_skip.reshape(1, w))


def _mem_kv_kernel(mem_ref, g_ref, w_ref, o_ref):
    o_ref[...] = _dot(_rms(mem_ref[...], g_ref[...]).astype(BF16), w_ref[...]).astype(o_ref.dtype)


def memory_kv(mem, gain, wkv, layer):
    nm, d = mem.shape
    n = wkv.shape[2]
    bn = _pow2_tile(STREAM_COLS, n)
    return pl.pallas_call(
        _mem_kv_kernel,
        grid=(n // bn,),
        in_specs=[pl.BlockSpec((nm, d), lambda j: (0, 0)),
                  pl.BlockSpec((1, d), lambda j: (0, 0)),
                  pl.BlockSpec((None, d, bn), lambda j: (layer, 0, j))],
        out_specs=pl.BlockSpec((nm, bn), lambda j: (0, j)),
        out_shape=jax.ShapeDtypeStruct((nm, n), BF16),
        compiler_params=_params("arbitrary"),
        name="memory_kv",
    )(mem, gain.reshape(1, d), wkv)


def _xattn_kernel(xn_ref, h_ref, wq_ref, kv_ref, wo_ref, gpost_ref, gpre_ref, h_out, a_out, y_even, y_odd):
    xw = XA_HEADS * XA_HEAD_DIM
    scale = XA_HEAD_DIM ** -0.5
    step = pl.program_id(0)

    @pl.when(step == 0)
    def _():
        y_odd[...] = jnp.zeros_like(y_odd)

    def phase(y_new, y_old):
        q = _dot(xn_ref[...], wq_ref[...]).astype(BF16)
        outs = []
        for hd in range(XA_HEADS):
            cols = slice(hd * XA_HEAD_DIM, (hd + 1) * XA_HEAD_DIM)
            s = _dot_nt(q[:, cols], kv_ref[:, cols]) * scale
            p = jnp.exp(s - jnp.max(s, axis=-1, keepdims=True))
            p = p / jnp.sum(p, axis=-1, keepdims=True)
            outs.append(_dot(p.astype(BF16), kv_ref[:, xw + hd * XA_HEAD_DIM:xw + (hd + 1) * XA_HEAD_DIM]))
        y_new[...] = _dot(jnp.concatenate(outs, axis=1).astype(BF16), wo_ref[...])
        hn = h_ref[...] + _rms(y_old[...], gpost_ref[...])
        h_out[...] = hn
        a_out[...] = _rms(hn, gpre_ref[...]).astype(a_out.dtype)

    @pl.when(step % 2 == 0)
    def _():
        phase(y_even, y_odd)

    @pl.when(step % 2 == 1)
    def _():
        phase(y_odd, y_even)


def cross_attention_block(xn, h, wq, kv, wo, layer, g_post, g_pre):
    m, d = h.shape
    nm, kvw = kv.shape
    xw = wq.shape[2]
    bm = _pow2_tile(ELEMENTWISE_ROWS, m)
    n_blocks = m // bm
    row = pl.BlockSpec((bm, d), lambda s: (jnp.maximum(s - 1, 0), 0))
    gain = pl.BlockSpec((1, d), lambda s: (0, 0))
    once = pl.Buffered(1)
    return pl.pallas_call(
        _xattn_kernel,
        grid=(n_blocks + 1,),
        in_specs=[pl.BlockSpec((bm, d), lambda s: (jnp.minimum(s, n_blocks - 1), 0)), row,
                  pl.BlockSpec((None, d, xw), lambda s: (layer, 0, 0), pipeline_mode=once),
                  pl.BlockSpec((nm, kvw), lambda s: (0, 0), pipeline_mode=once),
                  pl.BlockSpec((None, xw, d), lambda s: (layer, 0, 0), pipeline_mode=once),
                  gain, gain],
        out_specs=[row, row],
        out_shape=[jax.ShapeDtypeStruct((m, d), F32), jax.ShapeDtypeStruct((m, d), BF16)],
        scratch_shapes=[pltpu.VMEM((bm, d), F32), pltpu.VMEM((bm, d), F32)],
        compiler_params=_params("arbitrary"),
        name="cross_attention",
    )(xn, h, wq, kv, wo, g_post.reshape(1, d), g_pre.reshape(1, d))


def kernel(x, mem, positions, norm_gains, mem_norm_gain, ffn1_w_gu, ffn1_w_down, w_in, attn_sinks, w_attn_out, ssm_lambda_re, ssm_lambda_im, ssm_log_dt, ssm_b_re, ssm_b_im, ssm_c_re, ssm_c_im, ssm_d, w_glu, w_ssm_out, w_o, xa_wq, xa_wkv, xa_wo, ffn2_w_gu, ffn2_w_down):
    batch, l, d = x.shape
    assert batch == 1, "the kernels below walk one sequence"
    depth = norm_gains.shape[0]
    aw = w_attn_out.shape[1]
    sw = w_glu.shape[1]
    n_q = aw // HEAD_DIM
    kw = (w_in.shape[2] - aw - sw - 2 * d) // 2
    n_kv = kw // HEAD_DIM
    c_ssm = aw + 2 * kw
    c_gate = c_ssm + sw
    ssm_chunk = _pow2_tile(S5_CHUNK, l)
    sub_len = ssm_chunk // V7X_SUBLANES

    bf = lambda w: w.astype(BF16)
    ffn1_w_gu, ffn1_w_down, ffn2_w_gu, ffn2_w_down = bf(ffn1_w_gu), bf(ffn1_w_down), bf(ffn2_w_gu), bf(ffn2_w_down)
    w_in, w_attn_out, w_glu, w_ssm_out, w_o = bf(w_in), bf(w_attn_out), bf(w_glu), bf(w_ssm_out), bf(w_o)
    xa_wq, xa_wkv, xa_wo = bf(xa_wq), bf(xa_wkv), bf(xa_wo)

    h = x.reshape(l, d)
    mem2 = mem.reshape(mem.shape[1], d)
    tabs = rope_tables(positions.reshape(l))
    a = rmsnorm_bf16(h, norm_gains[0, 0])
    for i in range(depth):
        g = norm_gains[i]
        t = swiglu_up(a, ffn1_w_gu, i)
        h, u = matmul_residual_norm(t, ffn1_w_down, i, h, g[1], g[2], 0.5)
        qkv = matmul(u, w_in, i, 0, c_ssm, F32, "proj_qkv")
        s_in = matmul(u, w_in, i, c_ssm, sw, F32, "proj_ssm")
        gates = matmul(u, w_in, i, c_gate, 2 * d, F32, "proj_gates")
        attn_o = sliding_window_attention(qkv, attn_sinks[i], tabs, n_q, n_kv)
        prep = ssm_prepare(ssm_lambda_re[i], ssm_lambda_im[i], ssm_log_dt[i], ssm_b_re[i], ssm_b_im[i], sub_len)
        y_ssm = s5_scan(s_in, prep, ssm_c_re[i], ssm_c_im[i], ssm_d[i], ssm_chunk)
        glu_o = glu(y_ssm, w_glu, i)
        mixed = gated_mix(attn_o, glu_o, w_attn_out, w_ssm_out, i, gates)
        h, xn = matmul_residual_norm(mixed, w_o, i, h, g[3], g[4], 1.0)
        kv = memory_kv(mem2, mem_norm_gain[i], xa_wkv, i)
        h, a = cross_attention_block(xn, h, xa_wq, kv, xa_wo, i, g[5], g[6])
        t = swiglu_up(a, ffn2_w_gu, i)
        h, a = matmul_residual_norm(t, ffn2_w_down, i, h, g[7], norm_gains[i + 1, 0] if i + 1 < depth else None, 0.5)
    return h.reshape(batch, l, d)
```

```python
import functools
import math

import jax
import jax.numpy as jnp
from jax import lax
from jax.experimental import pallas as pl
from jax.experimental.pallas import tpu as pltpu

F32 = jnp.float32
BF16 = jnp.bfloat16

CHUNK = 64
WINDOW = 128
ATTN_BLOCK = 128
HEAD_DIM = 128
ROT_DIM = HEAD_DIM // 4
ROPE_THETA = 500000.0
SSM_GROUP_CH = 16
SSM_STATE = 64
XA_HEADS = 4
XA_HEAD_DIM = 128
RMS_EPS = 1e-6
MASK_VALUE = -1e30

V7X_LANES = 128
V7X_SUBLANES = 8
V7X_VMEM_LIMIT_BYTES = 52 * 1024 * 1024

GROUPS_PER_LANE_BLOCK = V7X_LANES // SSM_GROUP_CH
STATE_COLS = GROUPS_PER_LANE_BLOCK * SSM_STATE

STREAM_ROWS = 1024
STREAM_COLS = 512
WIDE_STREAM_COLS = 1024
RESIDENT_ROWS = 128
ELEMENTWISE_ROWS = 256
S5_CHUNK = 1024
S5_LANE_BLOCKS = 8


def _pow2_tile(limit, *sizes):
    t = 1
    while t * 2 <= limit and all(s % (t * 2) == 0 for s in sizes):
        t *= 2
    return t


def _params(*sem):
    return pltpu.CompilerParams(dimension_semantics=sem, vmem_limit_bytes=V7X_VMEM_LIMIT_BYTES)


def _rms(x, g):
    ms = jnp.mean(x * x, axis=-1, keepdims=True)
    return x * lax.rsqrt(ms + RMS_EPS) * g


def _dot(a, b):
    return jnp.dot(a, b, preferred_element_type=F32)


def _dot_nt(a, b):
    return lax.dot_general(a, b, (((1,), (1,)), ((), ())), preferred_element_type=F32)


def _layer_cols(layer, k, bn, off=0):
    return pl.BlockSpec((None, k, bn), lambda i, j: (layer, 0, j + off))


def _norm_kernel(x_ref, g_ref, o_ref):
    o_ref[...] = _rms(x_ref[...], g_ref[...]).astype(o_ref.dtype)


def rmsnorm_bf16(x, g):
    m, d = x.shape
    bm = _pow2_tile(ELEMENTWISE_ROWS, m)
    return pl.pallas_call(
        _norm_kernel,
        grid=(m // bm,),
        in_specs=[pl.BlockSpec((bm, d), lambda i: (i, 0)), pl.BlockSpec((1, d), lambda i: (0, 0))],
        out_specs=pl.BlockSpec((bm, d), lambda i: (i, 0)),
        out_shape=jax.ShapeDtypeStruct((m, d), BF16),
        compiler_params=_params("parallel"),
        name="rmsnorm",
    )(x, g.reshape(1, d))


def _mm_kernel(a_ref, w_ref, o_ref):
    o_ref[...] = _dot(a_ref[...], w_ref[0]).astype(o_ref.dtype)


def matmul(a, w, layer, col0, ncols, out_dtype, name):
    m, k = a.shape
    bm = _pow2_tile(STREAM_ROWS, m)
    bn = _pow2_tile(WIDE_STREAM_COLS, ncols)
    assert col0 % V7X_LANES == 0
    w_spec = pl.BlockSpec((pl.Element(1), pl.Element(k), pl.Element(bn)),
                          lambda i, j: (layer, 0, pl.multiple_of(col0 + j * bn, V7X_LANES)))
    return pl.pallas_call(
        _mm_kernel,
        grid=(m // bm, ncols // bn),
        in_specs=[pl.BlockSpec((bm, k), lambda i, j: (i, 0)), w_spec],
        out_specs=pl.BlockSpec((bm, bn), lambda i, j: (i, j)),
        out_shape=jax.ShapeDtypeStruct((m, ncols), out_dtype),
        compiler_params=_params("parallel", "arbitrary"),
        name=name,
    )(a, w)


def _swiglu_kernel(a_ref, wg_ref, wu_ref, o_ref):
    a = a_ref[...]
    gate = _dot(a, wg_ref[...])
    up = _dot(a, wu_ref[...])
    o_ref[...] = (gate * jax.nn.sigmoid(gate) * up).astype(o_ref.dtype)


def swiglu_up(a, w_gu, layer):
    m, k = a.shape
    f = w_gu.shape[2] // 2
    bm = _pow2_tile(STREAM_ROWS, m)
    bn = _pow2_tile(STREAM_COLS, f)
    return pl.pallas_call(
        _swiglu_kernel,
        grid=(m // bm, f // bn),
        in_specs=[pl.BlockSpec((bm, k), lambda i, j: (i, 0)),
                  _layer_cols(layer, k, bn), _layer_cols(layer, k, bn, f // bn)],
        out_specs=pl.BlockSpec((bm, bn), lambda i, j: (i, j)),
        out_shape=jax.ShapeDtypeStruct((m, f), BF16),
        compiler_params=_params("parallel", "arbitrary"),
        name="swiglu_up",
    )(a, w_gu, w_gu)


def _mm_res_norm_kernel(a_ref, w_ref, h_ref, gpost_ref, *rest, scale, with_norm):
    if with_norm:
        gpre_ref, h_out, a_out, y_even, y_odd = rest
    else:
        h_out, y_even, y_odd = rest
    s = pl.program_id(0)

    @pl.when(s == 0)
    def _():
        y_odd[...] = jnp.zeros_like(y_odd)

    def phase(y_new, y_old):
        y_new[...] = _dot(a_ref[...], w_ref[...])
        hn = h_ref[...] + scale * _rms(y_old[...], gpost_ref[...])
        h_out[...] = hn
        if with_norm:
            a_out[...] = _rms(hn, gpre_ref[...]).astype(a_out.dtype)

    @pl.when(s % 2 == 0)
    def _():
        phase(y_even, y_odd)

    @pl.when(s % 2 == 1)
    def _():
        phase(y_odd, y_even)


def matmul_residual_norm(a, w, layer, h, g_post, g_pre, scale):
    m, k = a.shape
    d = h.shape[1]
    bm = _pow2_tile(RESIDENT_ROWS, m)
    n_blocks = m // bm
    done = lambda s: (jnp.maximum(s - 1, 0), 0)
    row = pl.BlockSpec((bm, d), done)
    gain = pl.BlockSpec((1, d), lambda s: (0, 0))
    a_spec = pl.BlockSpec((bm, k), lambda s: (jnp.minimum(s, n_blocks - 1), 0))
    w_spec = pl.BlockSpec((None, k, d), lambda s: (layer, 0, 0), pipeline_mode=pl.Buffered(1))
    with_norm = g_pre is not None
    gains = [g_post.reshape(1, d)] + ([g_pre.reshape(1, d)] if with_norm else [])
    h_shape = jax.ShapeDtypeStruct((m, d), F32)
    out = pl.pallas_call(
        functools.partial(_mm_res_norm_kernel, scale=scale, with_norm=with_norm),
        grid=(n_blocks + 1,),
        in_specs=[a_spec, w_spec, row] + [gain] * len(gains),
        out_specs=[row, row] if with_norm else row,
        out_shape=[h_shape, jax.ShapeDtypeStruct((m, d), BF16)] if with_norm else h_shape,
        scratch_shapes=[pltpu.VMEM((bm, d), F32), pltpu.VMEM((bm, d), F32)],
        compiler_params=_params("arbitrary"),
        name="matmul_residual_norm",
    )(a, w, h, *gains)
    return out if with_norm else (out, None)


def _glu_kernel(y_ref, yj_ref, w_ref, o_ref):
    z = _dot(y_ref[...].astype(BF16), w_ref[...])
    o_ref[...] = (yj_ref[...] * jax.nn.sigmoid(z)).astype(o_ref.dtype)


def glu(y, w, layer):
    m, k = y.shape
    n = w.shape[2]
    bm = _pow2_tile(STREAM_ROWS, m)
    bn = _pow2_tile(WIDE_STREAM_COLS, n)
    return pl.pallas_call(
        _glu_kernel,
        grid=(m // bm, n // bn),
        in_specs=[pl.BlockSpec((bm, k), lambda i, j: (i, 0)),
                  pl.BlockSpec((bm, bn), lambda i, j: (i, j)),
                  _layer_cols(layer, k, bn)],
        out_specs=pl.BlockSpec((bm, bn), lambda i, j: (i, j)),
        out_shape=jax.ShapeDtypeStruct((m, n), BF16),
        compiler_params=_params("parallel", "arbitrary"),
        name="ssm_glu",
    )(y, y, w)


def _mix_kernel(ao_ref, gl_ref, wa_ref, ws_ref, ga_ref, gs_ref, o_ref):
    y_attn = _dot(ao_ref[...], wa_ref[...])
    y_ssm = _dot(gl_ref[...], ws_ref[...])
    o_ref[...] = (jax.nn.sigmoid(ga_ref[...]) * y_attn
                  + jax.nn.sigmoid(gs_ref[...]) * y_ssm).astype(o_ref.dtype)


def gated_mix(attn_o, glu_o, w_attn_out, w_ssm_out, layer, gates):
    m, ka = attn_o.shape
    ks = glu_o.shape[1]
    d = w_attn_out.shape[2]
    bm = _pow2_tile(STREAM_ROWS, m)
    bn = _pow2_tile(STREAM_COLS, d)
    return pl.pallas_call(
        _mix_kernel,
        grid=(m // bm, d // bn),
        in_specs=[pl.BlockSpec((bm, ka), lambda i, j: (i, 0)),
                  pl.BlockSpec((bm, ks), lambda i, j: (i, 0)),
                  _layer_cols(layer, ka, bn), _layer_cols(layer, ks, bn),
                  pl.BlockSpec((bm, bn), lambda i, j: (i, j)),
                  pl.BlockSpec((bm, bn), lambda i, j: (i, j + d // bn))],
        out_specs=pl.BlockSpec((bm, bn), lambda i, j: (i, j)),
        out_shape=jax.ShapeDtypeStruct((m, d), BF16),
        compiler_params=_params("parallel", "arbitrary"),
        name="gated_mix",
    )(attn_o, glu_o, w_attn_out, w_ssm_out, gates, gates)


def _rope_table_kernel(pos_ref, invf_ref, a_ref, b_ref, c_ref):
    ang = pos_ref[...].astype(F32) * invf_ref[...]
    cos, sin = jnp.cos(ang), jnp.sin(ang)
    lane = lax.broadcasted_iota(jnp.int32, ang.shape, 1)
    half = ROT_DIM // 2
    a_ref[...] = jnp.where(lane < ROT_DIM, cos, 1.0)
    b_ref[...] = jnp.where(lane < half, -sin, 0.0)
    c_ref[...] = jnp.where((lane >= half) & (lane < ROT_DIM), sin, 0.0)


def rope_tables(positions):
    l = positions.shape[0]
    inv_freq = ROPE_THETA ** (-jnp.arange(0, ROT_DIM, 2, dtype=F32) / ROT_DIM)
    invf = jnp.concatenate([inv_freq, inv_freq, jnp.zeros((HEAD_DIM - ROT_DIM,), F32)]).reshape(1, HEAD_DIM)
    bm = _pow2_tile(STREAM_ROWS, l)
    tab = pl.BlockSpec((bm, HEAD_DIM), lambda i: (i, 0))
    shape = jax.ShapeDtypeStruct((l, HEAD_DIM), F32)
    return pl.pallas_call(
        _rope_table_kernel,
        grid=(l // bm,),
        in_specs=[pl.BlockSpec((bm, 1), lambda i: (i, 0)), pl.BlockSpec((1, HEAD_DIM), lambda i: (0, 0))],
        out_specs=[tab, tab, tab],
        out_shape=[shape, shape, shape],
        compiler_params=_params("parallel"),
        name="rope_tables",
    )(positions.reshape(l, 1), invf)


def _rope(x, a, b, c):
    half = ROT_DIM // 2
    return x * a + pltpu.roll(x, HEAD_DIM - half, 1) * b + pltpu.roll(x, half, 1) * c


def _swa_kernel(sink_ref, q_ref, kp_ref, kc_ref, vp_ref, vc_ref, ap_ref, bp_ref, cp_ref,
                ac_ref, bc_ref, cc_ref, o_ref, *, n_kv, group):
    i = pl.program_id(0)
    blk = ATTN_BLOCK
    rows = group * blk
    ac, bc, cc = ac_ref[...], bc_ref[...], cc_ref[...]
    ap, bp, cp = ap_ref[...], bp_ref[...], cp_ref[...]

    q_idx = lax.broadcasted_iota(jnp.int32, (2 * blk, rows), 1) & (blk - 1)
    k_idx = lax.broadcasted_iota(jnp.int32, (2 * blk, rows), 0)
    chunk_shift = CHUNK.bit_length() - 1
    q_chunk = q_idx >> chunk_shift
    k_chunk = k_idx >> chunk_shift
    cpb = blk // CHUNK
    win = WINDOW // CHUNK
    band = (k_chunk >= q_chunk + cpb - win) & (k_chunk <= q_chunk + cpb)
    first_valid_key = jnp.where(i > 0, 0, blk)
    mask = band & (k_idx >= first_valid_key)
    scale = HEAD_DIM ** -0.5

    for kh in range(n_kv):
        ks = slice(kh * HEAD_DIM, (kh + 1) * HEAD_DIM)
        k = jnp.concatenate([_rope(kp_ref[:, ks], ap, bp, cp), _rope(kc_ref[:, ks], ac, bc, cc)], axis=0)
        v = jnp.concatenate([vp_ref[:, ks], vc_ref[:, ks]], axis=0).astype(BF16)
        heads = [kh * group + g for g in range(group)]
        q = jnp.concatenate([_rope(q_ref[:, h * HEAD_DIM:(h + 1) * HEAD_DIM], ac, bc, cc) for h in heads], axis=0)
        sink = jnp.concatenate([jnp.full((1, blk), sink_ref[0, h], F32) for h in heads], axis=1)
        s = _dot_nt(k.astype(BF16), q.astype(BF16)) * scale
        s = jnp.where(mask, s, MASK_VALUE)
        m = jnp.maximum(jnp.max(s, axis=0, keepdims=True), sink)
        p = jnp.exp(s - m)
        den = jnp.sum(p, axis=0, keepdims=True) + jnp.exp(sink - m)
        o = lax.dot_general((p / den).astype(BF16), v, (((0,), (0,)), ((), ())), preferred_element_type=F32)
        for g, h in enumerate(heads):
            o_ref[:, h * HEAD_DIM:(h + 1) * HEAD_DIM] = o[g * blk:(g + 1) * blk].astype(o_ref.dtype)


def sliding_window_attention(qkv, sinks, tabs, n_q, n_kv):
    l = qkv.shape[0]
    aw, kw = n_q * HEAD_DIM, n_kv * HEAD_DIM
    blk = ATTN_BLOCK
    k_col, v_col = aw // kw, aw // kw + 1
    assert aw % kw == 0
    prev = lambda i: jnp.maximum(i - 1, 0)
    tab_c = pl.BlockSpec((blk, HEAD_DIM), lambda i: (i, 0))
    tab_p = pl.BlockSpec((blk, HEAD_DIM), lambda i: (prev(i), 0))
    ta, tb, tc = tabs
    return pl.pallas_call(
        functools.partial(_swa_kernel, n_kv=n_kv, group=n_q // n_kv),
        grid=(l // blk,),
        in_specs=[pl.BlockSpec(memory_space=pltpu.SMEM),
                  pl.BlockSpec((blk, aw), lambda i: (i, 0)),
                  pl.BlockSpec((blk, kw), lambda i: (prev(i), k_col)),
                  pl.BlockSpec((blk, kw), lambda i: (i, k_col)),
                  pl.BlockSpec((blk, kw), lambda i: (prev(i), v_col)),
                  pl.BlockSpec((blk, kw), lambda i: (i, v_col)),
                  tab_p, tab_p, tab_p, tab_c, tab_c, tab_c],
        out_specs=pl.BlockSpec((blk, aw), lambda i: (i, 0)),
        out_shape=jax.ShapeDtypeStruct((l, aw), BF16),
        compiler_params=_params("parallel"),
        name="swa_attention",
    )(sinks.reshape(1, n_q), qkv, qkv, qkv, qkv, qkv, ta, tb, tc, ta, tb, tc)


def _ssm_prep_kernel(lre_ref, lim_ref, ldt_ref, bre_ref, bim_ref,
                     lbr_ref, lbi_ref, lsr_ref, lsi_ref, bbr_ref, bbi_ref, *, sub_len):
    lr = jnp.minimum(lre_ref[...], -1e-4)
    li = lim_ref[...]
    dt = jnp.exp(ldt_ref[...])
    ar, ai = lr * dt, li * dt
    mag = jnp.exp(ar)
    lbr, lbi = mag * jnp.cos(ai), mag * jnp.sin(ai)
    lbr_ref[...] = lbr
    lbi_ref[...] = lbi
    mag_s = jnp.exp(sub_len * ar)
    lsr_ref[...] = mag_s * jnp.cos(sub_len * ai)
    lsi_ref[...] = mag_s * jnp.sin(sub_len * ai)
    nr, ni = lbr - 1.0, lbi
    den = lr * lr + li * li
    cr = (nr * lr + ni * li) / den
    ci = (ni * lr - nr * li) / den
    bre, bim = bre_ref[...], bim_ref[...]
    bbr_ref[...] = cr * bre - ci * bim
    bbi_ref[...] = cr * bim + ci * bre


def ssm_prepare(lam_re, lam_im, log_dt, b_re, b_im, sub_len):
    g, p = lam_re.shape
    h = b_re.shape[-1]
    gp = jax.ShapeDtypeStruct((g, 1, p), F32)
    ghp = jax.ShapeDtypeStruct((g, h, p), F32)
    return pl.pallas_call(
        functools.partial(_ssm_prep_kernel, sub_len=float(sub_len)),
        out_shape=[gp, gp, gp, gp, ghp, ghp],
        name="ssm_prepare",
    )(lam_re.reshape(g, 1, p), lam_im.reshape(g, 1, p), log_dt.reshape(g, 1, 1),
      b_re.transpose(0, 2, 1), b_im.transpose(0, 2, 1))


def _block_diag(w):
    g, a, b = w.shape
    n = GROUPS_PER_LANE_BLOCK
    w4 = w.reshape(g // n, n, a, b)
    eye = jnp.eye(n, dtype=w.dtype)
    return (w4[:, :, :, None, :] * eye[None, :, None, :, None]).reshape(g // n, n * a, n * b)


def _gelu_tanh(x):
    return 0.5 * x * (1.0 + jnp.tanh(math.sqrt(2.0 / math.pi) * (x + 0.044715 * (x * x * x))))


def _ssm_kernel(u_ref, bmat_ref, cmat_ref, lam_ref, lams_ref, d_ref, y_ref,
                perm_ref, x_ref, carry_ref, *, chunk, lane_blocks):
    sub = V7X_SUBLANES
    sub_len = chunk // sub
    n = STATE_COLS

    @pl.when(pl.program_id(1) == 0)
    def _():
        carry_ref[...] = jnp.zeros_like(carry_ref)

    for lb in range(lane_blocks):
        lanes = slice(lb * V7X_LANES, (lb + 1) * V7X_LANES)
        perm_ref[...] = pltpu.einshape(
            "rjl->jrl", u_ref[:, lanes].reshape(sub, sub_len, V7X_LANES)).reshape(chunk, V7X_LANES)
        x_ref[...] = _dot(perm_ref[...].astype(BF16), bmat_ref[lb])

        lam_r = jnp.broadcast_to(lam_ref[lb, :, 0:n], (sub, n))
        lam_i = jnp.broadcast_to(lam_ref[lb, :, n:2 * n], (sub, n))

        def step(j, xr, xi):
            rows = pl.ds(j * sub, sub)
            bur, bui = x_ref[rows, 0:n], x_ref[rows, n:2 * n]
            return rows, lam_r * xr - lam_i * xi + bur, lam_r * xi + lam_i * xr + bui

        end_r = end_i = jnp.zeros((sub, n), F32)
        for j in range(sub_len):
            _, end_r, end_i = step(j, end_r, end_i)

        ls_r, ls_i = lams_ref[lb, :, 0:n], lams_ref[lb, :, n:2 * n]
        cr, ci = carry_ref[lb, :, 0:n], carry_ref[lb, :, n:2 * n]
        init_r, init_i = [], []
        for r in range(sub):
            init_r.append(cr)
            init_i.append(ci)
            cr, ci = (ls_r * cr - ls_i * ci + end_r[r:r + 1], ls_r * ci + ls_i * cr + end_i[r:r + 1])
        carry_ref[lb, :, 0:n] = cr
        carry_ref[lb, :, n:2 * n] = ci

        xr, xi = jnp.concatenate(init_r, axis=0), jnp.concatenate(init_i, axis=0)
        for j in range(sub_len):
            rows, xr, xi = step(j, xr, xi)
            x_ref[rows, 0:n] = xr
            x_ref[rows, n:2 * n] = xi

        y = _dot(x_ref[...].astype(BF16), cmat_ref[lb]) + d_ref[:, lanes] * perm_ref[...]
        y_ref[:, lanes] = pltpu.einshape(
            "jrl->rjl", _gelu_tanh(y).reshape(sub_len, sub, V7X_LANES)).reshape(chunk, V7X_LANES)


def s5_scan(u, prep, c_re, c_im, d_skip, chunk):
    l, w = u.shape
    lbr, lbi, lsr, lsi, bbr, bbi = prep
    g = lbr.shape[0]
    nb = g // GROUPS_PER_LANE_BLOCK
    n = STATE_COLS
    lane_blocks = _pow2_tile(S5_LANE_BLOCKS, nb)
    wide = lane_blocks * V7X_LANES
    bmat = jnp.concatenate([_block_diag(bbr), _block_diag(bbi)], axis=-1).astype(BF16)
    cmat = jnp.concatenate([_block_diag(c_re.transpose(0, 2, 1)),
                            _block_diag(-c_im.transpose(0, 2, 1))], axis=1).astype(BF16)
    lam = jnp.concatenate([lbr.reshape(nb, 1, n), lbi.reshape(nb, 1, n)], axis=-1)
    lams = jnp.concatenate([lsr.reshape(nb, 1, n), lsi.reshape(nb, 1, n)], axis=-1)
    per_block = lambda *tail: pl.BlockSpec((lane_blocks,) + tail, lambda b, c: (b, 0, 0))
    return pl.pallas_call(
        functools.partial(_ssm_kernel, chunk=chunk, lane_blocks=lane_blocks),
        grid=(nb // lane_blocks, l // chunk),
        in_specs=[pl.BlockSpec((chunk, wide), lambda b, c: (c, b)),
                  per_block(V7X_LANES, 2 * n), per_block(2 * n, V7X_LANES),
                  per_block(1, 2 * n), per_block(1, 2 * n),
                  pl.BlockSpec((1, wide), lambda b, c: (0, b))],
        out_specs=pl.BlockSpec((chunk, wide), lambda b, c: (c, b)),
        out_shape=jax.ShapeDtypeStruct((l, w), F32),
        scratch_shapes=[pltpu.VMEM((chunk, V7X_LANES), F32),
                        pltpu.VMEM((chunk, 2 * n), F32),
                        pltpu.VMEM((lane_blocks, 1, 2 * n), F32)],
        compiler_params=_params("parallel", "arbitrary"),
        name="s5_scan",
    )(u, bmat, cmat, lam, lams, d_skip.reshape(1, w))


def _mem_kv_kernel(mem_ref, g_ref, w_ref, o_ref):
    o_ref[...] = _dot(_rms(mem_ref[...], g_ref[...]).astype(BF16), w_ref[...]).astype(o_ref.dtype)


def memory_kv(mem, gain, wkv, layer):
    nm, d = mem.shape
    n = wkv.shape[2]
    bn = _pow2_tile(STREAM_COLS, n)
    return pl.pallas_call(
        _mem_kv_kernel,
        grid=(n // bn,),
        in_specs=[pl.BlockSpec((nm, d), lambda j: (0, 0)),
                  pl.BlockSpec((1, d), lambda j: (0, 0)),
                  pl.BlockSpec((None, d, bn), lambda j: (layer, 0, j))],
        out_specs=pl.BlockSpec((nm, bn), lambda j: (0, j)),
        out_shape=jax.ShapeDtypeStruct((nm, n), BF16),
        compiler_params=_params("arbitrary"),
        name="memory_kv",
    )(mem, gain.reshape(1, d), wkv)


def _xattn_kernel(xn_ref, h_ref, wq_ref, kv_ref, wo_ref, gpost_ref, gpre_ref, h_out, a_out, y_even, y_odd):
    xw = XA_HEADS * XA_HEAD_DIM
    scale = XA_HEAD_DIM ** -0.5
    step = pl.program_id(0)

    @pl.when(step == 0)
    def _():
        y_odd[...] = jnp.zeros_like(y_odd)

    def phase(y_new, y_old):
        q = _dot(xn_ref[...], wq_ref[...]).astype(BF16)
        outs = []
        for hd in range(XA_HEADS):
            cols = slice(hd * XA_HEAD_DIM, (hd + 1) * XA_HEAD_DIM)
            s = _dot_nt(q[:, cols], kv_ref[:, cols]) * scale
            p = jnp.exp(s - jnp.max(s, axis=-1, keepdims=True))
            p = p / jnp.sum(p, axis=-1, keepdims=True)
            outs.append(_dot(p.astype(BF16), kv_ref[:, xw + hd * XA_HEAD_DIM:xw + (hd + 1) * XA_HEAD_DIM]))
        y_new[...] = _dot(jnp.concatenate(outs, axis=1).astype(BF16), wo_ref[...])
        hn = h_ref[...] + _rms(y_old[...], gpost_ref[...])
        h_out[...] = hn
        a_out[...] = _rms(hn, gpre_ref[...]).astype(a_out.dtype)

    @pl.when(step % 2 == 0)
    def _():
        phase(y_even, y_odd)

    @pl.when(step % 2 == 1)
    def _():
        phase(y_odd, y_even)


def cross_attention_block(xn, h, wq, kv, wo, layer, g_post, g_pre):
    m, d = h.shape
    nm, kvw = kv.shape
    xw = wq.shape[2]
    bm = _pow2_tile(ELEMENTWISE_ROWS, m)
    n_blocks = m // bm
    row = pl.BlockSpec((bm, d), lambda s: (jnp.maximum(s - 1, 0), 0))
    gain = pl.BlockSpec((1, d), lambda s: (0, 0))
    once = pl.Buffered(1)
    return pl.pallas_call(
        _xattn_kernel,
        grid=(n_blocks + 1,),
        in_specs=[pl.BlockSpec((bm, d), lambda s: (jnp.minimum(s, n_blocks - 1), 0)), row,
                  pl.BlockSpec((None, d, xw), lambda s: (layer, 0, 0), pipeline_mode=once),
                  pl.BlockSpec((nm, kvw), lambda s: (0, 0), pipeline_mode=once),
                  pl.BlockSpec((None, xw, d), lambda s: (layer, 0, 0), pipeline_mode=once),
                  gain, gain],
        out_specs=[row, row],
        out_shape=[jax.ShapeDtypeStruct((m, d), F32), jax.ShapeDtypeStruct((m, d), BF16)],
        scratch_shapes=[pltpu.VMEM((bm, d), F32), pltpu.VMEM((bm, d), F32)],
        compiler_params=_params("arbitrary"),
        name="cross_attention",
    )(xn, h, wq, kv, wo, g_post.reshape(1, d), g_pre.reshape(1, d))


def kernel(x, mem, positions, norm_gains, mem_norm_gain, ffn1_w_gu, ffn1_w_down, w_in, attn_sinks, w_attn_out, ssm_lambda_re, ssm_lambda_im, ssm_log_dt, ssm_b_re, ssm_b_im, ssm_c_re, ssm_c_im, ssm_d, w_glu, w_ssm_out, w_o, xa_wq, xa_wkv, xa_wo, ffn2_w_gu, ffn2_w_down):
    batch, l, d = x.shape
    assert batch == 1, "the kernels below walk one sequence"
    depth = norm_gains.shape[0]
    aw = w_attn_out.shape[1]
    sw = w_glu.shape[1]
    n_q = aw // HEAD_DIM
    kw = (w_in.shape[2] - aw - sw - 2 * d) // 2
    n_kv = kw // HEAD_DIM
    c_ssm = aw + 2 * kw
    c_gate = c_ssm + sw
    ssm_chunk = _pow2_tile(S5_CHUNK, l)
    sub_len = ssm_chunk // V7X_SUBLANES

    bf = lambda w: w.astype(BF16)
    ffn1_w_gu, ffn1_w_down, ffn2_w_gu, ffn2_w_down = bf(ffn1_w_gu), bf(ffn1_w_down), bf(ffn2_w_gu), bf(ffn2_w_down)
    w_in, w_attn_out, w_glu, w_ssm_out, w_o = bf(w_in), bf(w_attn_out), bf(w_glu), bf(w_ssm_out), bf(w_o)
    xa_wq, xa_wkv, xa_wo = bf(xa_wq), bf(xa_wkv), bf(xa_wo)

    h = x.reshape(l, d)
    mem2 = mem.reshape(mem.shape[1], d)
    tabs = rope_tables(positions.reshape(l))
    a = rmsnorm_bf16(h, norm_gains[0, 0])
    for i in range(depth):
        g = norm_gains[i]
        t = swiglu_up(a, ffn1_w_gu, i)
        h, u = matmul_residual_norm(t, ffn1_w_down, i, h, g[1], g[2], 0.5)
        qkv = matmul(u, w_in, i, 0, c_ssm, F32, "proj_qkv")
        s_in = matmul(u, w_in, i, c_ssm, sw, F32, "proj_ssm")
        gates = matmul(u, w_in, i, c_gate, 2 * d, F32, "proj_gates")
        attn_o = sliding_window_attention(qkv, attn_sinks[i], tabs, n_q, n_kv)
        prep = ssm_prepare(ssm_lambda_re[i], ssm_lambda_im[i], ssm_log_dt[i], ssm_b_re[i], ssm_b_im[i], sub_len)
        y_ssm = s5_scan(s_in, prep, ssm_c_re[i], ssm_c_im[i], ssm_d[i], ssm_chunk)
        glu_o = glu(y_ssm, w_glu, i)
        mixed = gated_mix(attn_o, glu_o, w_attn_out, w_ssm_out, i, gates)
        h, xn = matmul_residual_norm(mixed, w_o, i, h, g[3], g[4], 1.0)
        kv = memory_kv(mem2, mem_norm_gain[i], xa_wkv, i)
        h, a = cross_attention_block(xn, h, xa_wq, kv, xa_wo, i, g[5], g[6])
        t = swiglu_up(a, ffn2_w_gu, i)
        h, a = matmul_residual_norm(t, ffn2_w_down, i, h, g[7], norm_gains[i + 1, 0] if i + 1 < depth else None, 0.5)
    return h.reshape(batch, l, d)
```

```python
import functools
import math

import jax
import jax.numpy as jnp
from jax import lax
from jax.experimental import pallas as pl
from jax.experimental.pallas import tpu as pltpu

F32 = jnp.float32
BF16 = jnp.bfloat16

CHUNK = 64
WINDOW = 128
ATTN_BLOCK = 128
HEAD_DIM = 128
ROT_DIM = HEAD_DIM // 4
ROPE_THETA = 500000.0
SSM_GROUP_CH = 16
SSM_STATE = 64
XA_HEADS = 4
XA_HEAD_DIM = 128
RMS_EPS = 1e-6
MASK_VALUE = -1e30

V7X_LANES = 128
V7X_SUBLANES = 8
V7X_VMEM_LIMIT_BYTES = 52 * 1024 * 1024

GROUPS_PER_LANE_BLOCK = V7X_LANES // SSM_GROUP_CH
STATE_COLS = GROUPS_PER_LANE_BLOCK * SSM_STATE

STREAM_ROWS = 1024
STREAM_COLS = 512
WIDE_STREAM_COLS = 1024
RESIDENT_ROWS = 128
ELEMENTWISE_ROWS = 256
S5_CHUNK = 1024
S5_LANE_BLOCKS = 8


def _pow2_tile(limit, *sizes):
    t = 1
    while t * 2 <= limit and all(s % (t * 2) == 0 for s in sizes):
        t *= 2
    return t


def _params(*sem):
    return pltpu.CompilerParams(dimension_semantics=sem, vmem_limit_bytes=V7X_VMEM_LIMIT_BYTES)


def _rms(x, g):
    ms = jnp.mean(x * x, axis=-1, keepdims=True)
    return x * lax.rsqrt(ms + RMS_EPS) * g


def _dot(a, b):
    return jnp.dot(a, b, preferred_element_type=F32)


def _dot_nt(a, b):
    return lax.dot_general(a, b, (((1,), (1,)), ((), ())), preferred_element_type=F32)


def _layer_cols(layer, k, bn, off=0):
    return pl.BlockSpec((None, k, bn), lambda i, j: (layer, 0, j + off))


def _norm_kernel(x_ref, g_ref, o_ref):
    o_ref[...] = _rms(x_ref[...], g_ref[...]).astype(o_ref.dtype)


def rmsnorm_bf16(x, g):
    m, d = x.shape
    bm = _pow2_tile(ELEMENTWISE_ROWS, m)
    return pl.pallas_call(
        _norm_kernel,
        grid=(m // bm,),
        in_specs=[pl.BlockSpec((bm, d), lambda i: (i, 0)), pl.BlockSpec((1, d), lambda i: (0, 0))],
        out_specs=pl.BlockSpec((bm, d), lambda i: (i, 0)),
        out_shape=jax.ShapeDtypeStruct((m, d), BF16),
        compiler_params=_params("parallel"),
        name="rmsnorm",
    )(x, g.reshape(1, d))


def _mm_kernel(a_ref, w_ref, o_ref):
    o_ref[...] = _dot(a_ref[...], w_ref[0]).astype(o_ref.dtype)


def matmul(a, w, layer, col0, ncols, out_dtype, name):
    m, k = a.shape
    bm = _pow2_tile(STREAM_ROWS, m)
    bn = _pow2_tile(WIDE_STREAM_COLS, ncols)
    assert col0 % V7X_LANES == 0
    w_spec = pl.BlockSpec((pl.Element(1), pl.Element(k), pl.Element(bn)),
                          lambda i, j: (layer, 0, pl.multiple_of(col0 + j * bn, V7X_LANES)))
    return pl.pallas_call(
        _mm_kernel,
        grid=(m // bm, ncols // bn),
        in_specs=[pl.BlockSpec((bm, k), lambda i, j: (i, 0)), w_spec],
        out_specs=pl.BlockSpec((bm, bn), lambda i, j: (i, j)),
        out_shape=jax.ShapeDtypeStruct((m, ncols), out_dtype),
        compiler_params=_params("parallel", "arbitrary"),
        name=name,
    )(a, w)


def _swiglu_kernel(a_ref, wg_ref, wu_ref, o_ref):
    a = a_ref[...]
    gate = _dot(a, wg_ref[...])
    up = _dot(a, wu_ref[...])
    o_ref[...] = (gate * jax.nn.sigmoid(gate) * up).astype(o_ref.dtype)


def swiglu_up(a, w_gu, layer):
    m, k = a.shape
    f = w_gu.shape[2] // 2
    bm = _pow2_tile(STREAM_ROWS, m)
    bn = _pow2_tile(STREAM_COLS, f)
    return pl.pallas_call(
        _swiglu_kernel,
        grid=(m // bm, f // bn),
        in_specs=[pl.BlockSpec((bm, k), lambda i, j: (i, 0)),
                  _layer_cols(layer, k, bn), _layer_cols(layer, k, bn, f // bn)],
        out_specs=pl.BlockSpec((bm, bn), lambda i, j: (i, j)),
        out_shape=jax.ShapeDtypeStruct((m, f), BF16),
        compiler_params=_params("parallel", "arbitrary"),
        name="swiglu_up",
    )(a, w_gu, w_gu)


def _mm_res_norm_kernel(a_ref, w_ref, h_ref, gpost_ref, *rest, scale, with_norm):
    if with_norm:
        gpre_ref, h_out, a_out, y_even, y_odd = rest
    else:
        h_out, y_even, y_odd = rest
    s = pl.program_id(0)

    @pl.when(s == 0)
    def _():
        y_odd[...] = jnp.zeros_like(y_odd)

    def phase(y_new, y_old):
        y_new[...] = _dot(a_ref[...], w_ref[...])
        hn = h_ref[...] + scale * _rms(y_old[...], gpost_ref[...])
        h_out[...] = hn
        if with_norm:
            a_out[...] = _rms(hn, gpre_ref[...]).astype(a_out.dtype)

    @pl.when(s % 2 == 0)
    def _():
        phase(y_even, y_odd)

    @pl.when(s % 2 == 1)
    def _():
        phase(y_odd, y_even)


def matmul_residual_norm(a, w, layer, h, g_post, g_pre, scale):
    m, k = a.shape
    d = h.shape[1]
    bm = _pow2_tile(RESIDENT_ROWS, m)
    n_blocks = m // bm
    done = lambda s: (jnp.maximum(s - 1, 0), 0)
    row = pl.BlockSpec((bm, d), done)
    gain = pl.BlockSpec((1, d), lambda s: (0, 0))
    a_spec = pl.BlockSpec((bm, k), lambda s: (jnp.minimum(s, n_blocks - 1), 0))
    w_spec = pl.BlockSpec((None, k, d), lambda s: (layer, 0, 0), pipeline_mode=pl.Buffered(1))
    with_norm = g_pre is not None
    gains = [g_post.reshape(1, d)] + ([g_pre.reshape(1, d)] if with_norm else [])
    h_shape = jax.ShapeDtypeStruct((m, d), F32)
    out = pl.pallas_call(
        functools.partial(_mm_res_norm_kernel, scale=scale, with_norm=with_norm),
        grid=(n_blocks + 1,),
        in_specs=[a_spec, w_spec, row] + [gain] * len(gains),
        out_specs=[row, row] if with_norm else row,
        out_shape=[h_shape, jax.ShapeDtypeStruct((m, d), BF16)] if with_norm else h_shape,
        scratch_shapes=[pltpu.VMEM((bm, d), F32), pltpu.VMEM((bm, d), F32)],
        compiler_params=_params("arbitrary"),
        name="matmul_residual_norm",
    )(a, w, h, *gains)
    return out if with_norm else (out, None)


def _glu_kernel(y_ref, yj_ref, w_ref, o_ref):
    z = _dot(y_ref[...].astype(BF16), w_ref[...])
    o_ref[...] = (yj_ref[...] * jax.nn.sigmoid(z)).astype(o_ref.dtype)


def glu(y, w, layer):
    m, k = y.shape
    n = w.shape[2]
    bm = _pow2_tile(STREAM_ROWS, m)
    bn = _pow2_tile(WIDE_STREAM_COLS, n)
    return pl.pallas_call(
        _glu_kernel,
        grid=(m // bm, n // bn),
        in_specs=[pl.BlockSpec((bm, k), lambda i, j: (i, 0)),
                  pl.BlockSpec((bm, bn), lambda i, j: (i, j)),
                  _layer_cols(layer, k, bn)],
        out_specs=pl.BlockSpec((bm, bn), lambda i, j: (i, j)),
        out_shape=jax.ShapeDtypeStruct((m, n), BF16),
        compiler_params=_params("parallel", "arbitrary"),
        name="ssm_glu",
    )(y, y, w)


def _mix_kernel(ao_ref, gl_ref, wa_ref, ws_ref, ga_ref, gs_ref, o_ref):
    y_attn = _dot(ao_ref[...], wa_ref[...])
    y_ssm = _dot(gl_ref[...], ws_ref[...])
    o_ref[...] = (jax.nn.sigmoid(ga_ref[...]) * y_attn
                  + jax.nn.sigmoid(gs_ref[...]) * y_ssm).astype(o_ref.dtype)


def gated_mix(attn_o, glu_o, w_attn_out, w_ssm_out, layer, gates):
    m, ka = attn_o.shape
    ks = glu_o.shape[1]
    d = w_attn_out.shape[2]
    bm = _pow2_tile(STREAM_ROWS, m)
    bn = _pow2_tile(STREAM_COLS, d)
    return pl.pallas_call(
        _mix_kernel,
        grid=(m // bm, d // bn),
        in_specs=[pl.BlockSpec((bm, ka), lambda i, j: (i, 0)),
                  pl.BlockSpec((bm, ks), lambda i, j: (i, 0)),
                  _layer_cols(layer, ka, bn), _layer_cols(layer, ks, bn),
                  pl.BlockSpec((bm, bn), lambda i, j: (i, j)),
                  pl.BlockSpec((bm, bn), lambda i, j: (i, j + d // bn))],
        out_specs=pl.BlockSpec((bm, bn), lambda i, j: (i, j)),
        out_shape=jax.ShapeDtypeStruct((m, d), BF16),
        compiler_params=_params("parallel", "arbitrary"),
        name="gated_mix",
    )(attn_o, glu_o, w_attn_out, w_ssm_out, gates, gates)


def _rope_table_kernel(pos_ref, invf_ref, a_ref, b_ref, c_ref):
    ang = pos_ref[...].astype(F32) * invf_ref[...]
    cos, sin = jnp.cos(ang), jnp.sin(ang)
    lane = lax.broadcasted_iota(jnp.int32, ang.shape, 1)
    half = ROT_DIM // 2
    a_ref[...] = jnp.where(lane < ROT_DIM, cos, 1.0)
    b_ref[...] = jnp.where(lane < half, -sin, 0.0)
    c_ref[...] = jnp.where((lane >= half) & (lane < ROT_DIM), sin, 0.0)


def rope_tables(positions):
    l = positions.shape[0]
    inv_freq = ROPE_THETA ** (-jnp.arange(0, ROT_DIM, 2, dtype=F32) / ROT_DIM)
    invf = jnp.concatenate([inv_freq, inv_freq, jnp.zeros((HEAD_DIM - ROT_DIM,), F32)]).reshape(1, HEAD_DIM)
    bm = _pow2_tile(STREAM_ROWS, l)
    tab = pl.BlockSpec((bm, HEAD_DIM), lambda i: (i, 0))
    shape = jax.ShapeDtypeStruct((l, HEAD_DIM), F32)
    return pl.pallas_call(
        _rope_table_kernel,
        grid=(l // bm,),
        in_specs=[pl.BlockSpec((bm, 1), lambda i: (i, 0)), pl.BlockSpec((1, HEAD_DIM), lambda i: (0, 0))],
        out_specs=[tab, tab, tab],
        out_shape=[shape, shape, shape],
        compiler_params=_params("parallel"),
        name="rope_tables",
    )(positions.reshape(l, 1), invf)


def _rope(x, a, b, c):
    half = ROT_DIM // 2
    return x * a + pltpu.roll(x, HEAD_DIM - half, 1) * b + pltpu.roll(x, half, 1) * c


def _swa_kernel(sink_ref, q_ref, kp_ref, kc_ref, vp_ref, vc_ref, ap_ref, bp_ref, cp_ref,
                ac_ref, bc_ref, cc_ref, o_ref, *, n_kv, group):
    i = pl.program_id(0)
    blk = ATTN_BLOCK
    rows = group * blk

    q_idx = lax.broadcasted_iota(jnp.int32, (2 * blk, rows), 1) & (blk - 1)
    k_idx = lax.broadcasted_iota(jnp.int32, (2 * blk, rows), 0)
    chunk_shift = CHUNK.bit_length() - 1
    q_chunk = q_idx >> chunk_shift
    k_chunk = k_idx >> chunk_shift
    cpb = blk // CHUNK
    win = WINDOW // CHUNK
    band = (k_chunk >= q_chunk + cpb - win) & (k_chunk <= q_chunk + cpb)
    first_valid_key = jnp.where(i > 0, 0, blk)
    masks = [band & (k_idx >= first_valid_key), band]
    scale = HEAD_DIM ** -0.5
    halves = [slice(0, blk), slice(blk, 2 * blk)]
    tabs = [(ac_ref[hs, :], bc_ref[hs, :], cc_ref[hs, :]) for hs in halves]

    for kh in range(n_kv):
        ks = slice(kh * HEAD_DIM, (kh + 1) * HEAD_DIM)
        k_blocks = [_rope(kp_ref[:, ks], ap_ref[...], bp_ref[...], cp_ref[...]).astype(BF16)]
        k_blocks += [_rope(kc_ref[hs, ks], *tabs[t]).astype(BF16) for t, hs in enumerate(halves)]
        v_blocks = [vp_ref[:, ks].astype(BF16)] + [vc_ref[hs, ks].astype(BF16) for hs in halves]
        heads = [kh * group + g for g in range(group)]
        sink = jnp.concatenate([jnp.full((1, blk), sink_ref[0, h], F32) for h in heads], axis=1)
        for t, hs in enumerate(halves):
            k = jnp.concatenate(k_blocks[t:t + 2], axis=0)
            v = jnp.concatenate(v_blocks[t:t + 2], axis=0)
            q = jnp.concatenate([_rope(q_ref[hs, h * HEAD_DIM:(h + 1) * HEAD_DIM], *tabs[t]) for h in heads], axis=0)
            s = _dot_nt(k, q.astype(BF16)) * scale
            s = jnp.where(masks[t], s, MASK_VALUE)
            m = jnp.maximum(jnp.max(s, axis=0, keepdims=True), sink)
            p = jnp.exp(s - m)
            den = jnp.sum(p, axis=0, keepdims=True) + jnp.exp(sink - m)
            o = lax.dot_general((p / den).astype(BF16), v, (((0,), (0,)), ((), ())), preferred_element_type=F32)
            for g, h in enumerate(heads):
                o_ref[hs, h * HEAD_DIM:(h + 1) * HEAD_DIM] = o[g * blk:(g + 1) * blk].astype(o_ref.dtype)


def sliding_window_attention(qkv, sinks, tabs, n_q, n_kv):
    l = qkv.shape[0]
    aw, kw = n_q * HEAD_DIM, n_kv * HEAD_DIM
    blk = ATTN_BLOCK
    pair = 2 * blk
    k_col, v_col = aw // kw, aw // kw + 1
    assert aw % kw == 0 and l % pair == 0
    prev = lambda i: jnp.maximum(2 * i - 1, 0)
    tab_c = pl.BlockSpec((pair, HEAD_DIM), lambda i: (i, 0))
    tab_p = pl.BlockSpec((blk, HEAD_DIM), lambda i: (prev(i), 0))
    ta, tb, tc = tabs
    return pl.pallas_call(
        functools.partial(_swa_kernel, n_kv=n_kv, group=n_q // n_kv),
        grid=(l // pair,),
        in_specs=[pl.BlockSpec(memory_space=pltpu.SMEM),
                  pl.BlockSpec((pair, aw), lambda i: (i, 0)),
                  pl.BlockSpec((blk, kw), lambda i: (prev(i), k_col)),
                  pl.BlockSpec((pair, kw), lambda i: (i, k_col)),
                  pl.BlockSpec((blk, kw), lambda i: (prev(i), v_col)),
                  pl.BlockSpec((pair, kw), lambda i: (i, v_col)),
                  tab_p, tab_p, tab_p, tab_c, tab_c, tab_c],
        out_specs=pl.BlockSpec((pair, aw), lambda i: (i, 0)),
        out_shape=jax.ShapeDtypeStruct((l, aw), BF16),
        compiler_params=_params("parallel"),
        name="swa_attention",
    )(sinks.reshape(1, n_q), qkv, qkv, qkv, qkv, qkv, ta, tb, tc, ta, tb, tc)


def _ssm_prep_kernel(lre_ref, lim_ref, ldt_ref, bre_ref, bim_ref,
                     lbr_ref, lbi_ref, lsr_ref, lsi_ref, bbr_ref, bbi_ref, *, sub_len):
    lr = jnp.minimum(lre_ref[...], -1e-4)
    li = lim_ref[...]
    dt = jnp.exp(ldt_ref[...])
    ar, ai = lr * dt, li * dt
    mag = jnp.exp(ar)
    lbr, lbi = mag * jnp.cos(ai), mag * jnp.sin(ai)
    lbr_ref[...] = lbr
    lbi_ref[...] = lbi
    mag_s = jnp.exp(sub_len * ar)
    lsr_ref[...] = mag_s * jnp.cos(sub_len * ai)
    lsi_ref[...] = mag_s * jnp.sin(sub_len * ai)
    nr, ni = lbr - 1.0, lbi
    den = lr * lr + li * li
    cr = (nr * lr + ni * li) / den
    ci = (ni * lr - nr * li) / den
    bre, bim = bre_ref[...], bim_ref[...]
    bbr_ref[...] = cr * bre - ci * bim
    bbi_ref[...] = cr * bim + ci * bre


def ssm_prepare(lam_re, lam_im, log_dt, b_re, b_im, sub_len):
    g, p = lam_re.shape
    h = b_re.shape[-1]
    gp = jax.ShapeDtypeStruct((g, 1, p), F32)
    ghp = jax.ShapeDtypeStruct((g, h, p), F32)
    return pl.pallas_call(
        functools.partial(_ssm_prep_kernel, sub_len=float(sub_len)),
        out_shape=[gp, gp, gp, gp, ghp, ghp],
        name="ssm_prepare",
    )(lam_re.reshape(g, 1, p), lam_im.reshape(g, 1, p), log_dt.reshape(g, 1, 1),
      b_re.transpose(0, 2, 1), b_im.transpose(0, 2, 1))


def _block_diag(w):
    g, a, b = w.shape
    n = GROUPS_PER_LANE_BLOCK
    w4 = w.reshape(g // n, n, a, b)
    eye = jnp.eye(n, dtype=w.dtype)
    return (w4[:, :, :, None, :] * eye[None, :, None, :, None]).reshape(g // n, n * a, n * b)


def _gelu_tanh(x):
    return 0.5 * x * (1.0 + jnp.tanh(math.sqrt(2.0 / math.pi) * (x + 0.044715 * (x * x * x))))


def _ssm_kernel(u_ref, bmat_ref, cmat_ref, lam_ref, lams_ref, d_ref, y_ref,
                perm_ref, x_ref, carry_ref, *, chunk, lane_blocks):
    sub = V7X_SUBLANES
    sub_len = chunk // sub
    n = STATE_COLS

    @pl.when(pl.program_id(1) == 0)
    def _():
        carry_ref[...] = jnp.zeros_like(carry_ref)

    for lb in range(lane_blocks):
        lanes = slice(lb * V7X_LANES, (lb + 1) * V7X_LANES)
        perm_ref[...] = pltpu.einshape(
            "rjl->jrl", u_ref[:, lanes].reshape(sub, sub_len, V7X_LANES)).reshape(chunk, V7X_LANES)
        x_ref[...] = _dot(perm_ref[...].astype(BF16), bmat_ref[lb])

        lam_r = jnp.broadcast_to(lam_ref[lb, :, 0:n], (sub, n))
        lam_i = jnp.broadcast_to(lam_ref[lb, :, n:2 * n], (sub, n))

        def step(j, xr, xi):
            rows = pl.ds(j * sub, sub)
            bur, bui = x_ref[rows, 0:n], x_ref[rows, n:2 * n]
            return rows, lam_r * xr - lam_i * xi + bur, lam_r * xi + lam_i * xr + bui

        end_r = end_i = jnp.zeros((sub, n), F32)
        for j in range(sub_len):
            _, end_r, end_i = step(j, end_r, end_i)

        ls_r, ls_i = lams_ref[lb, :, 0:n], lams_ref[lb, :, n:2 * n]
        cr, ci = carry_ref[lb, :, 0:n], carry_ref[lb, :, n:2 * n]
        init_r, init_i = [], []
        for r in range(sub):
            init_r.append(cr)
            init_i.append(ci)
            cr, ci = (ls_r * cr - ls_i * ci + end_r[r:r + 1], ls_r * ci + ls_i * cr + end_i[r:r + 1])
        carry_ref[lb, :, 0:n] = cr
        carry_ref[lb, :, n:2 * n] = ci

        xr, xi = jnp.concatenate(init_r, axis=0), jnp.concatenate(init_i, axis=0)
        for j in range(sub_len):
            rows, xr, xi = step(j, xr, xi)
            x_ref[rows, 0:n] = xr
            x_ref[rows, n:2 * n] = xi

        y = _dot(x_ref[...].astype(BF16), cmat_ref[lb]) + d_ref[:, lanes] * perm_ref[...]
        y_ref[:, lanes] = pltpu.einshape(
            "jrl->rjl", _gelu_tanh(y).reshape(sub_len, sub, V7X_LANES)).reshape(chunk, V7X_LANES)


def s5_scan(u, prep, c_re, c_im, d_skip, chunk):
    l, w = u.shape
    lbr, lbi, lsr, lsi, bbr, bbi = prep
    g = lbr.shape[0]
    nb = g // GROUPS_PER_LANE_BLOCK
    n = STATE_COLS
    lane_blocks = _pow2_tile(S5_LANE_BLOCKS, nb)
    wide = lane_blocks * V7X_LANES
    bmat = jnp.concatenate([_block_diag(bbr), _block_diag(bbi)], axis=-1).astype(BF16)
    cmat = jnp.concatenate([_block_diag(c_re.transpose(0, 2, 1)),
                            _block_diag(-c_im.transpose(0, 2, 1))], axis=1).astype(BF16)
    lam = jnp.concatenate([lbr.reshape(nb, 1, n), lbi.reshape(nb, 1, n)], axis=-1)
    lams = jnp.concatenate([lsr.reshape(nb, 1, n), lsi.reshape(nb, 1, n)], axis=-1)
    per_block = lambda *tail: pl.BlockSpec((lane_blocks,) + tail, lambda b, c: (b, 0, 0))
    return pl.pallas_call(
        functools.partial(_ssm_kernel, chunk=chunk, lane_blocks=lane_blocks),
        grid=(nb // lane_blocks, l // chunk),
        in_specs=[pl.BlockSpec((chunk, wide), lambda b, c: (c, b)),
                  per_block(V7X_LANES, 2 * n), per_block(2 * n, V7X_LANES),
                  per_block(1, 2 * n), per_block(1, 2 * n),
                  pl.BlockSpec((1, wide), lambda b, c: (0, b))],
        out_specs=pl.BlockSpec((chunk, wide), lambda b, c: (c, b)),
        out_shape=jax.ShapeDtypeStruct((l, w), F32),
        scratch_shapes=[pltpu.VMEM((chunk, V7X_LANES), F32),
                        pltpu.VMEM((chunk, 2 * n), F32),
                        pltpu.VMEM((lane_blocks, 1, 2 * n), F32)],
        compiler_params=_params("parallel", "arbitrary"),
        name="s5_scan",
    )(u, bmat, cmat, lam, lams, d_skip.reshape(1, w))


def _mem_kv_kernel(mem_ref, g_ref, w_ref, o_ref):
    o_ref[...] = _dot(_rms(mem_ref[...], g_ref[...]).astype(BF16), w_ref[...]).astype(o_ref.dtype)


def memory_kv(mem, gain, wkv, layer):
    nm, d = mem.shape
    n = wkv.shape[2]
    bn = _pow2_tile(STREAM_COLS, n)
    return pl.pallas_call(
        _mem_kv_kernel,
        grid=(n // bn,),
        in_specs=[pl.BlockSpec((nm, d), lambda j: (0, 0)),
                  pl.BlockSpec((1, d), lambda j: (0, 0)),
                  pl.BlockSpec((None, d, bn), lambda j: (layer, 0, j))],
        out_specs=pl.BlockSpec((nm, bn), lambda j: (0, j)),
        out_shape=jax.ShapeDtypeStruct((nm, n), BF16),
        compiler_params=_params("arbitrary"),
        name="memory_kv",
    )(mem, gain.reshape(1, d), wkv)


def _xattn_kernel(xn_ref, h_ref, wq_ref, kv_ref, wo_ref, gpost_ref, gpre_ref, h_out, a_out, y_even, y_odd):
    xw = XA_HEADS * XA_HEAD_DIM
    scale = XA_HEAD_DIM ** -0.5
    step = pl.program_id(0)

    @pl.when(step == 0)
    def _():
        y_odd[...] = jnp.zeros_like(y_odd)

    def phase(y_new, y_old):
        q = _dot(xn_ref[...], wq_ref[...]).astype(BF16)
        outs = []
        for hd in range(XA_HEADS):
            cols = slice(hd * XA_HEAD_DIM, (hd + 1) * XA_HEAD_DIM)
            s = _dot_nt(q[:, cols], kv_ref[:, cols]) * scale
            p = jnp.exp(s - jnp.max(s, axis=-1, keepdims=True))
            p = p / jnp.sum(p, axis=-1, keepdims=True)
            outs.append(_dot(p.astype(BF16), kv_ref[:, xw + hd * XA_HEAD_DIM:xw + (hd + 1) * XA_HEAD_DIM]))
        y_new[...] = _dot(jnp.concatenate(outs, axis=1).astype(BF16), wo_ref[...])
        hn = h_ref[...] + _rms(y_old[...], gpost_ref[...])
        h_out[...] = hn
        a_out[...] = _rms(hn, gpre_ref[...]).astype(a_out.dtype)

    @pl.when(step % 2 == 0)
    def _():
        phase(y_even, y_odd)

    @pl.when(step % 2 == 1)
    def _():
        phase(y_odd, y_even)


def cross_attention_block(xn, h, wq, kv, wo, layer, g_post, g_pre):
    m, d = h.shape
    nm, kvw = kv.shape
    xw = wq.shape[2]
    bm = _pow2_tile(ELEMENTWISE_ROWS, m)
    n_blocks = m // bm
    row = pl.BlockSpec((bm, d), lambda s: (jnp.maximum(s - 1, 0), 0))
    gain = pl.BlockSpec((1, d), lambda s: (0, 0))
    once = pl.Buffered(1)
    return pl.pallas_call(
        _xattn_kernel,
        grid=(n_blocks + 1,),
        in_specs=[pl.BlockSpec((bm, d), lambda s: (jnp.minimum(s, n_blocks - 1), 0)), row,
                  pl.BlockSpec((None, d, xw), lambda s: (layer, 0, 0), pipeline_mode=once),
                  pl.BlockSpec((nm, kvw), lambda s: (0, 0), pipeline_mode=once),
                  pl.BlockSpec((None, xw, d), lambda s: (layer, 0, 0), pipeline_mode=once),
                  gain, gain],
        out_specs=[row, row],
        out_shape=[jax.ShapeDtypeStruct((m, d), F32), jax.ShapeDtypeStruct((m, d), BF16)],
        scratch_shapes=[pltpu.VMEM((bm, d), F32), pltpu.VMEM((bm, d), F32)],
        compiler_params=_params("arbitrary"),
        name="cross_attention",
    )(xn, h, wq, kv, wo, g_post.reshape(1, d), g_pre.reshape(1, d))


def kernel(x, mem, positions, norm_gains, mem_norm_gain, ffn1_w_gu, ffn1_w_down, w_in, attn_sinks, w_attn_out, ssm_lambda_re, ssm_lambda_im, ssm_log_dt, ssm_b_re, ssm_b_im, ssm_c_re, ssm_c_im, ssm_d, w_glu, w_ssm_out, w_o, xa_wq, xa_wkv, xa_wo, ffn2_w_gu, ffn2_w_down):
    batch, l, d = x.shape
    assert batch == 1, "the kernels below walk one sequence"
    depth = norm_gains.shape[0]
    aw = w_attn_out.shape[1]
    sw = w_glu.shape[1]
    n_q = aw // HEAD_DIM
    kw = (w_in.shape[2] - aw - sw - 2 * d) // 2
    n_kv = kw // HEAD_DIM
    c_ssm = aw + 2 * kw
    c_gate = c_ssm + sw
    ssm_chunk = _pow2_tile(S5_CHUNK, l)
    sub_len = ssm_chunk // V7X_SUBLANES

    bf = lambda w: w.astype(BF16)
    ffn1_w_gu, ffn1_w_down, ffn2_w_gu, ffn2_w_down = bf(ffn1_w_gu), bf(ffn1_w_down), bf(ffn2_w_gu), bf(ffn2_w_down)
    w_in, w_attn_out, w_glu, w_ssm_out, w_o = bf(w_in), bf(w_attn_out), bf(w_glu), bf(w_ssm_out), bf(w_o)
    xa_wq, xa_wkv, xa_wo = bf(xa_wq), bf(xa_wkv), bf(xa_wo)

    h = x.reshape(l, d)
    mem2 = mem.reshape(mem.shape[1], d)
    tabs = rope_tables(positions.reshape(l))
    a = rmsnorm_bf16(h, norm_gains[0, 0])
    for i in range(depth):
        g = norm_gains[i]
        t = swiglu_up(a, ffn1_w_gu, i)
        h, u = matmul_residual_norm(t, ffn1_w_down, i, h, g[1], g[2], 0.5)
        qkv = matmul(u, w_in, i, 0, c_ssm, F32, "proj_qkv")
        s_in = matmul(u, w_in, i, c_ssm, sw, F32, "proj_ssm")
        gates = matmul(u, w_in, i, c_gate, 2 * d, F32, "proj_gates")
        attn_o = sliding_window_attention(qkv, attn_sinks[i], tabs, n_q, n_kv)
        prep = ssm_prepare(ssm_lambda_re[i], ssm_lambda_im[i], ssm_log_dt[i], ssm_b_re[i], ssm_b_im[i], sub_len)
        y_ssm = s5_scan(s_in, prep, ssm_c_re[i], ssm_c_im[i], ssm_d[i], ssm_chunk)
        glu_o = glu(y_ssm, w_glu, i)
        mixed = gated_mix(attn_o, glu_o, w_attn_out, w_ssm_out, i, gates)
        h, xn = matmul_residual_norm(mixed, w_o, i, h, g[3], g[4], 1.0)
        kv = memory_kv(mem2, mem_norm_gain[i], xa_wkv, i)
        h, a = cross_attention_block(xn, h, xa_wq, kv, xa_wo, i, g[5], g[6])
        t = swiglu_up(a, ffn2_w_gu, i)
        h, a = matmul_residual_norm(t, ffn2_w_down, i, h, g[7], norm_gains[i + 1, 0] if i + 1 < depth else None, 0.5)
    return h.reshape(batch, l, d)
```
